```python
import math
import jax, jax.numpy as jnp
from jax import lax
import numpy as np

D_MODEL = 2048
BATCH = 1
SEQ = 8192
DEPTH = 4

CHUNK = 64
N_MIXERS = 3
Q_BLOCK = 128
TOKEN_BLOCK = 128
EPS = 1e-6

CONV_WIDTH = 31

HG_HEADS = 16
HG_DK = D_MODEL // HG_HEADS
HG_DV = D_MODEL // HG_HEADS

MLA_HEADS = 16
Q_LORA = 512
KV_LORA = 512
QK_NOPE = 128
QK_ROPE = 64
V_HEAD = 128
ROPE_THETA = 10000.0

PEER_HEADS = 8
N_KEYS = 128
N_EXPERTS = N_KEYS * N_KEYS
PEER_DK = 256
PEER_TOPK = 16

N_CONV = (DEPTH + 2) // 3
N_HGRN = (DEPTH + 1) // 3
N_MLA = DEPTH // 3

kernel_name = "hybrid_conv_hgrn2_mla_peer_trunk"

F32 = jnp.float32


def rms_norm(x, g):
    xf = x.astype(F32)
    y = xf * lax.rsqrt(jnp.mean(xf * xf, axis=-1, keepdims=True) + EPS)
    return (y * g.astype(F32)).astype(x.dtype)


def layer_norm(x, g, b):
    xf = x.astype(F32)
    mu = jnp.mean(xf, axis=-1, keepdims=True)
    var = jnp.mean(jnp.square(xf - mu), axis=-1, keepdims=True)
    y = (xf - mu) * lax.rsqrt(var + EPS)
    return (y * g.astype(F32) + b.astype(F32)).astype(x.dtype)


def causal_depthwise_conv(x, w, b):
    y = lax.conv_general_dilated(
        x, w[:, None, :], window_strides=(1,), padding=[(CONV_WIDTH - 1, 0)],
        dimension_numbers=('NWC', 'WIO', 'NWC'), feature_group_count=x.shape[-1])
    return y + b


def conformer_conv(h, w_pw1, b_pw1, w_dw, b_dw, ln_g, ln_b, w_pw2, b_pw2):
    y = h @ w_pw1 + b_pw1
    val, gate = jnp.split(y, 2, axis=-1)
    y = val * jax.nn.sigmoid(gate)
    y = causal_depthwise_conv(y, w_dw, b_dw)
    y = jax.nn.silu(layer_norm(y, ln_g, ln_b))
    return y @ w_pw2 + b_pw2


def hgrn_lower_bounds(logits):
    p = jax.nn.softmax(logits.astype(F32), axis=0)
    return jnp.cumsum(p, axis=0) - p[0]


def gla_chunk_scan(q, k, v, logf):
    _, B, H, C, DK = q.shape
    DV = v.shape[-1]
    causal = jnp.tril(jnp.ones((C, C), dtype=bool))[:, :, None]

    def step(state, inp):
        qc, kc, vc, lc = inp
        b = jnp.cumsum(lc, axis=-2)
        diff = b[..., :, None, :] - b[..., None, :, :]
        decay = jnp.exp(jnp.where(causal, diff, -jnp.inf))
        attn = jnp.einsum('bhtd,bhsd,bhtsd->bhts', qc, kc, decay)
        o = (jnp.einsum('bhts,bhsv->bhtv', attn, vc)
             + jnp.einsum('bhtd,bhdv->bhtv', qc * jnp.exp(b), state))
        b_last = b[..., -1:, :]
        state = (jnp.exp(b_last[..., 0, :])[..., None] * state
                 + jnp.einsum('bhsd,bhsv->bhdv', kc * jnp.exp(b_last - b), vc))
        return state, o

    s0 = jnp.zeros((B, H, DK, DV), F32)
    _, o = lax.scan(step, s0, (q, k, v, logf))
    return o


def hgrn2_mixer(h, w_in, lb, norm_g, w_out):
    B, S, _ = h.shape
    q, f, i, g = jnp.split(h @ w_in, 4, axis=-1)
    lb = lb.astype(F32)
    logf = jnp.logaddexp(jnp.log(lb), jnp.log1p(-lb) + jax.nn.log_sigmoid(f.astype(F32)))
    k = -jnp.expm1(logf)

    def heads(t, d):
        return t.reshape(B, S // CHUNK, CHUNK, HG_HEADS, d).transpose(1, 0, 3, 2, 4)

    qc = heads(q.astype(F32) * (HG_DK ** -0.5), HG_DK)
    o = gla_chunk_scan(qc, heads(k, HG_DK), heads(i.astype(F32), HG_DV), heads(logf, HG_DK))
    o = o.transpose(1, 0, 3, 2, 4).reshape(B, S, HG_HEADS, HG_DV)
    o = rms_norm(o, norm_g) * jax.nn.silu(g.astype(F32).reshape(B, S, HG_HEADS, HG_DV))
    return o.reshape(B, S, D_MODEL).astype(h.dtype) @ w_out


def rope_tables(pos):
    inv_freq = ROPE_THETA ** (-jnp.arange(0, QK_ROPE, 2, dtype=F32) / QK_ROPE)
    ang = pos.astype(F32)[..., None] * inv_freq
    return jnp.cos(ang)[:, :, None, :], jnp.sin(ang)[:, :, None, :]


def apply_rope(x, cos, sin):
    x1, x2 = jnp.split(x.astype(F32), 2, axis=-1)
    return jnp.concatenate([x1 * cos - x2 * sin, x2 * cos + x1 * sin], axis=-1).astype(x.dtype)


def chunk_causal_attention(q, k, v):
    B, S, H, DH = q.shape
    nb = S // Q_BLOCK
    scale = DH ** -0.5
    key_chunk = jnp.arange(S) // CHUNK
    qb = q.reshape(B, nb, Q_BLOCK, H, DH).transpose(1, 0, 2, 3, 4)

    def one_block(args):
        q_blk, blk = args
        q_chunk = (blk * Q_BLOCK + jnp.arange(Q_BLOCK)) // CHUNK
        s = jnp.einsum('bqhd,bkhd->bhqk', q_blk, k, preferred_element_type=F32) * scale
        s = jnp.where(key_chunk[None, :] <= q_chunk[:, None], s, -jnp.inf)
        p = jax.nn.softmax(s, axis=-1)
        return jnp.einsum('bhqk,bkhd->bqhd', p.astype(v.dtype), v)

    o = lax.map(one_block, (qb, jnp.arange(nb)))
    return o.transpose(1, 0, 2, 3, 4).reshape(B, S, H, v.shape[-1])


def mla_mixer(h, pos, w_in, q_norm_g, kv_norm_g, w_uq, w_ukv, w_o):
    B, S, _ = h.shape
    c_q, c_kv, k_rope = jnp.split(h @ w_in, [Q_LORA, Q_LORA + KV_LORA], axis=-1)
    q = (rms_norm(c_q, q_norm_g) @ w_uq).reshape(B, S, MLA_HEADS, QK_NOPE + QK_ROPE)
    kv = (rms_norm(c_kv, kv_norm_g) @ w_ukv).reshape(B, S, MLA_HEADS, QK_NOPE + V_HEAD)
    q_nope, q_rope = jnp.split(q, [QK_NOPE], axis=-1)
    k_nope, v = jnp.split(kv, [QK_NOPE], axis=-1)
    cos, sin = rope_tables(pos)
    q_rope = apply_rope(q_rope, cos, sin)
    k_rope = apply_rope(k_rope[:, :, None, :], cos, sin)
    q = jnp.concatenate([q_nope, q_rope], axis=-1)
    k = jnp.concatenate([k_nope, jnp.broadcast_to(k_rope, (B, S, MLA_HEADS, QK_ROPE))], axis=-1)
    o = chunk_causal_attention(q, k, v)
    return o.reshape(B, S, MLA_HEADS * V_HEAD) @ w_o


def peer_ffn(h, w_q, sub_keys, u, v):
    B, S, D = h.shape
    q = (h @ w_q).astype(F32).reshape(B, S, PEER_HEADS, 2, PEER_DK // 2)
    scores = jnp.einsum('bshpd,hpnd->bshpn', q, sub_keys.astype(F32))
    top_s, top_i = lax.top_k(scores, PEER_TOPK)
    cand_s = top_s[..., 0, :, None] + top_s[..., 1, None, :]
    cand_i = top_i[..., 0, :, None] * N_KEYS + top_i[..., 1, None, :]
    cand_s = cand_s.reshape(B, S, PEER_HEADS, PEER_TOPK * PEER_TOPK)
    cand_i = cand_i.reshape(B, S, PEER_HEADS, PEER_TOPK * PEER_TOPK)
    best_s, best_pos = lax.top_k(cand_s, PEER_TOPK)
    expert_idx = jnp.take_along_axis(cand_i, best_pos, axis=-1)
    gates = jax.nn.softmax(best_s, axis=-1)

    T = B * S
    nblk = T // TOKEN_BLOCK
    E = PEER_HEADS * PEER_TOPK
    h_b = h.reshape(nblk, TOKEN_BLOCK, D)
    idx_b = expert_idx.reshape(nblk, TOKEN_BLOCK, E)
    g_b = gates.reshape(nblk, TOKEN_BLOCK, E)

    def block(args):
        h_blk, idx_blk, g_blk = args
        u_sel = jnp.take(u, idx_blk, axis=0)
        a = jnp.einsum('td,ted->te', h_blk, u_sel, preferred_element_type=F32)
        w = (jax.nn.gelu(a) * g_blk).astype(v.dtype)
        v_sel = jnp.take(v, idx_blk, axis=0)
        return jnp.einsum('te,ted->td', w, v_sel)

    out = lax.map(block, (h_b, idx_b, g_b))
    return out.reshape(B, S, D)


def setup_inputs(seed: int = 0) -> dict:
    key = jax.random.key(seed)
    ks = iter(jax.random.split(key, 40))
    D = D_MODEL

    def nrm(shape, scale):
        return jax.random.normal(next(ks), shape, F32) * scale

    def gain(shape):
        return 1.0 + nrm(shape, 0.02)

    x = nrm((BATCH, SEQ, D), 1.0)
    c = nrm((BATCH, D), 1.0)
    offsets = jax.random.randint(next(ks), (BATCH, 1), 0, 4096, dtype=jnp.int32)
    positions = offsets + jnp.arange(SEQ, dtype=jnp.int32)[None, :]

    inp = {
        "x": x, "c": c, "positions": positions,
        "ada_w": nrm((DEPTH, D, 6 * D), 0.5 * D ** -0.5),
        "ada_b": nrm((DEPTH, 6 * D), 0.01),
        "norm_g": gain((DEPTH, 2, D)),
        "conv_w_pw1": nrm((N_CONV, D, 2 * D), D ** -0.5),
        "conv_b_pw1": nrm((N_CONV, 2 * D), 0.01),
        "conv_w_dw": nrm((N_CONV, CONV_WIDTH, D), CONV_WIDTH ** -0.5),
        "conv_b_dw": nrm((N_CONV, D), 0.01),
        "conv_ln_g": gain((N_CONV, D)),
        "conv_ln_b": nrm((N_CONV, D), 0.01),
        "conv_w_pw2": nrm((N_CONV, D, D), D ** -0.5),
        "conv_b_pw2": nrm((N_CONV, D), 0.01),
        "hg_w_in": nrm((N_HGRN, D, 4 * D), D ** -0.5),
        "hg_lb_logits": nrm((DEPTH, D), 0.1),
        "hg_norm_g": gain((N_HGRN, HG_DV)),
        "hg_w_out": nrm((N_HGRN, D, D), D ** -0.5),
        "mla_w_in": nrm((N_MLA, D, Q_LORA + KV_LORA + QK_ROPE), D ** -0.5),
        "mla_q_norm_g": gain((N_MLA, Q_LORA)),
        "mla_kv_norm_g": gain((N_MLA, KV_LORA)),
        "mla_w_uq": nrm((N_MLA, Q_LORA, MLA_HEADS * (QK_NOPE + QK_ROPE)), Q_LORA ** -0.5),
        "mla_w_ukv": nrm((N_MLA, KV_LORA, MLA_HEADS * (QK_NOPE + V_HEAD)), KV_LORA ** -0.5),
        "mla_w_o": nrm((N_MLA, MLA_HEADS * V_HEAD, D), (MLA_HEADS * V_HEAD) ** -0.5),
        "peer_w_q": nrm((DEPTH, D, PEER_HEADS * PEER_DK), D ** -0.5),
        "peer_sub_keys": nrm((DEPTH, PEER_HEADS, 2, N_KEYS, PEER_DK // 2), (PEER_DK // 2) ** -0.5),
        "peer_u": nrm((DEPTH, N_EXPERTS, D), D ** -0.5),
        "peer_v": nrm((DEPTH, N_EXPERTS, D), PEER_TOPK ** -0.5),
        "final_g": gain((D,)),
    }
    return inp


def reference(x, c, positions, ada_w, ada_b, norm_g,
              conv_w_pw1, conv_b_pw1, conv_w_dw, conv_b_dw, conv_ln_g, conv_ln_b, conv_w_pw2, conv_b_pw2,
              hg_w_in, hg_lb_logits, hg_norm_g, hg_w_out,
              mla_w_in, mla_q_norm_g, mla_kv_norm_g, mla_w_uq, mla_w_ukv, mla_w_o,
              peer_w_q, peer_sub_keys, peer_u, peer_v, final_g):
    lb_all = hgrn_lower_bounds(hg_lb_logits)
    cond = jax.nn.silu(c)
    h = x
    for i in range(DEPTH):
        mod = (cond @ ada_w[i] + ada_b[i])[:, None, :]
        sh1, sc1, g1, sh2, sc2, g2 = jnp.split(mod, 6, axis=-1)
        kind = i % N_MIXERS
        slot = i // N_MIXERS

        u = rms_norm(h, norm_g[i, 0]) * (1.0 + sc1) + sh1
        if kind == 0:
            y = conformer_conv(u, conv_w_pw1[slot], conv_b_pw1[slot], conv_w_dw[slot], conv_b_dw[slot],
                               conv_ln_g[slot], conv_ln_b[slot], conv_w_pw2[slot], conv_b_pw2[slot])
        elif kind == 1:
            y = hgrn2_mixer(u, hg_w_in[slot], lb_all[i], hg_norm_g[slot], hg_w_out[slot])
        else:
            y = mla_mixer(u, positions, mla_w_in[slot], mla_q_norm_g[slot], mla_kv_norm_g[slot],
                          mla_w_uq[slot], mla_w_ukv[slot], mla_w_o[slot])
        h = h + g1 * y

        u = rms_norm(h, norm_g[i, 1]) * (1.0 + sc2) + sh2
        h = h + g2 * peer_ffn(u, peer_w_q[i], peer_sub_keys[i], peer_u[i], peer_v[i])
    return rms_norm(h, final_g)
```

```python
import functools
import math

import jax
import jax.numpy as jnp
from jax import lax
from jax.experimental import pallas as pl
from jax.experimental.pallas import tpu as pltpu

F32 = jnp.float32
BF16 = jnp.bfloat16

CHUNK = 64
EPS = 1e-6
CONV_WIDTH = 31
HG_HEADS = 16
HG_DK = 128
MLA_HEADS = 16
Q_LORA = 512
KV_LORA = 512
QK_NOPE = 128
QK_ROPE = 64
V_HEAD = 128
ROPE_THETA = 10000.0
PEER_HEADS = 8
N_KEYS = 128
PEER_TOPK = 16

LANES = 128
SUBLANES = 8
MXU_DIM = 256
VMEM_LIMIT_BYTES = 56 * 1024 * 1024

NPASS_MIXER = 3
NPASS_ROUTE = 3
SUB = 16

NN = (((1,), (0,)), ((), ()))
NT = (((1,), (1,)), ((), ()))
TN = (((0,), (0,)), ((), ()))


def _params(sem):
    return pltpu.CompilerParams(dimension_semantics=sem, vmem_limit_bytes=VMEM_LIMIT_BYTES)


def _split(x, n):
    hi = x.astype(BF16)
    if n == 1:
        return (hi,)
    r = x - hi.astype(F32)
    mid = r.astype(BF16)
    if n == 2:
        return (hi, mid)
    lo = (r - mid.astype(F32)).astype(BF16)
    return (hi, mid, lo)


def _nparts(npass):
    return 1 if npass == 1 else 2


def _mm(a_parts, b_parts, dn, npass):
    order = 0 if npass == 1 else 1
    out = None
    for i, a in enumerate(a_parts):
        for j, b in enumerate(b_parts):
            if i + j > order:
                continue
            t = lax.dot_general(a, b, dn, preferred_element_type=F32)
            out = t if out is None else out + t
    return out


def _rms(x, g):
    return x * lax.rsqrt(jnp.mean(x * x, axis=-1, keepdims=True) + EPS) * g


def _norm_mod(x, g, sc, sh):
    return _rms(x, g) * (1.0 + sc) + sh


def _mod_kernel(c_ref, w_ref, b_ref, o_ref):
    c = c_ref[...]
    cond = c * jax.nn.sigmoid(c)
    o_ref[0] = jnp.sum(w_ref[0] * cond, axis=0, keepdims=True) + b_ref[0]


def _ada_mod(c, ada_w, ada_b):
    depth, d, n = ada_w.shape
    tn = 1536 if n % 1536 == 0 else n
    return pl.pallas_call(
        _mod_kernel,
        grid=(depth, n // tn),
        in_specs=[pl.BlockSpec((d, 1), lambda i, j: (0, 0)),
                  pl.BlockSpec((1, d, tn), lambda i, j: (i, 0, j)),
                  pl.BlockSpec((1, 1, tn), lambda i, j: (i, 0, j))],
        out_specs=pl.BlockSpec((1, 1, tn), lambda i, j: (i, 0, j)),
        out_shape=jax.ShapeDtypeStruct((depth, 1, n), F32),
        compiler_params=_params(("arbitrary", "arbitrary")),
    )(c.reshape(d, 1), ada_w, ada_b.reshape(depth, 1, n))


def _linear_kernel(*refs, n_pro, n_w, n_epi, n_out, n_pscr, npass, pro_fn, epi_fn, emit_hi):
    np_ = _nparts(npass)
    pro_refs = refs[:n_pro]
    w_refs = refs[n_pro:n_pro + n_w]
    epi_refs = refs[n_pro + n_w:n_pro + n_w + n_epi]
    out_refs = refs[n_pro + n_w + n_epi:n_pro + n_w + n_epi + n_out]
    rest = refs[n_pro + n_w + n_epi + n_out:]
    hi_ref = None
    if emit_hi:
        hi_ref, rest = rest[0], rest[1:]
    u_scr = rest[:np_]
    p_scr = rest[np_:np_ + n_pscr]

    row_tile = pl.program_id(0)

    @pl.when(pl.program_id(1) == 0)
    def _():
        u = pro_fn(row_tile, *pro_refs, *p_scr)
        parts = _split(u, np_)
        for s, p in zip(u_scr, parts):
            s[...] = p
        if emit_hi:
            hi_ref[...] = parts[0]

    a_parts = tuple(s[...] for s in u_scr)
    accs = [_mm(a_parts, _split(w[...], np_), NN, npass) for w in w_refs]
    outs = epi_fn(accs, *epi_refs)
    for o_ref, o in zip(out_refs, outs):
        o_ref[...] = o


def _fused_linear(*, S, K, N, tm, tn, npass, pro_fn, pro_in, w_in, epi_fn, epi_in,
                  n_out=1, emit_hi=False, pro_scratch=()):
    np_ = _nparts(npass)
    arrays = [a for a, _ in pro_in] + [a for a, _ in w_in] + [a for a, _ in epi_in]
    specs = [s for _, s in pro_in] + [s for _, s in w_in] + [s for _, s in epi_in]
    out_shape = [jax.ShapeDtypeStruct((S, N), F32)] * n_out
    out_specs = [pl.BlockSpec((tm, tn), lambda i, j: (i, j))] * n_out
    if emit_hi:
        out_shape = out_shape + [jax.ShapeDtypeStruct((S, K), BF16)]
        out_specs = out_specs + [pl.BlockSpec((tm, K), lambda i, j: (i, 0))]
    kern = functools.partial(
        _linear_kernel, n_pro=len(pro_in), n_w=len(w_in), n_epi=len(epi_in), n_out=n_out,
        n_pscr=len(pro_scratch), npass=npass, pro_fn=pro_fn, epi_fn=epi_fn, emit_hi=emit_hi)
    return pl.pallas_call(
        kern,
        grid=(S // tm, N // tn),
        in_specs=specs,
        out_specs=out_specs,
        out_shape=out_shape,
        scratch_shapes=[pltpu.VMEM((tm, K), BF16)] * np_ + list(pro_scratch),
        compiler_params=_params(("arbitrary", "arbitrary")),
    )(*arrays)


def _row_spec(tm, k):
    return pl.BlockSpec((tm, k), lambda i, j: (i, 0))


def _vec_spec(k):
    return pl.BlockSpec((1, k), lambda i, j: (0, 0))


def _col_vec_spec(tn, off=0):
    return pl.BlockSpec((1, tn), lambda i, j: (0, j + off))


def _w_spec(k, tn, off=0):
    return pl.BlockSpec((k, tn), lambda i, j: (0, j + off))


def _tile_spec(tm, tn):
    return pl.BlockSpec((tm, tn), lambda i, j: (i, j))


def _pro_norm_mod(row_tile, x_ref, g_ref, sc_ref, sh_ref):
    return _norm_mod(x_ref[...], g_ref[...], sc_ref[...], sh_ref[...])


def _pro_identity(row_tile, x_ref):
    return x_ref[...]


def _epi_plain(accs):
    return (accs[0],)


def _epi_residual(accs, h_ref, g_ref):
    return (h_ref[...] + g_ref[...] * accs[0],)


def _epi_residual_bias(accs, b_ref, h_ref, g_ref):
    return (h_ref[...] + g_ref[...] * (accs[0] + b_ref[...]),)


def _epi_glu(accs, b1_ref, b2_ref):
    return ((accs[0] + b1_ref[...]) * jax.nn.sigmoid(accs[1] + b2_ref[...]),)


def _tiles(S, N):
    tm = min(512, S)
    tn = 512 if N % 512 == 0 else N
    return tm, tn


def _norm_mod_inputs(h, g, sc, sh, tm):
    d = h.shape[1]
    return [(h, _row_spec(tm, d)), (g.reshape(1, d), _vec_spec(d)), (sc, _vec_spec(d)), (sh, _vec_spec(d))]


CONV_HALO = 32
CONV_ROWS = 64


def _pro_conv(row_tile, z_ref, halo_ref, wdw_ref, bdw_ref, lng_ref, lnb_ref, zbuf, cbuf):
    tm, d = z_ref.shape
    first = row_tile == 0
    zbuf[0:CONV_HALO, :] = jnp.where(first, 0.0, halo_ref[...])
    zbuf[CONV_HALO:, :] = z_ref[...]
    lead = CONV_HALO - (CONV_WIDTH - 1)

    def col_body(cb, carry):
        c0 = pl.multiple_of(cb * LANES, LANES)
        cols = pl.ds(c0, LANES)
        for rc in range(tm // CONV_ROWS):
            acc = jnp.zeros((CONV_ROWS, LANES), F32)
            for w in range(CONV_WIDTH):
                acc = acc + zbuf[pl.ds(rc * CONV_ROWS + lead + w, CONV_ROWS), cols] * wdw_ref[pl.ds(w, 1), cols]
            cbuf[pl.ds(rc * CONV_ROWS, CONV_ROWS), cols] = acc + bdw_ref[:, cols]
        return carry

    lax.fori_loop(0, d // LANES, col_body, 0)
    y = cbuf[...]
    mu = jnp.mean(y, axis=-1, keepdims=True)
    var = jnp.mean(jnp.square(y - mu), axis=-1, keepdims=True)
    y = (y - mu) * lax.rsqrt(var + EPS) * lng_ref[...] + lnb_ref[...]
    return y * jax.nn.sigmoid(y)


def _conv_mixer(h, norm_g, sc, sh, gate, w_pw1, b_pw1, w_dw, b_dw, ln_g, ln_b, w_pw2, b_pw2):
    S, D = h.shape
    tm, tn = _tiles(S, D)
    nb = D // tn
    z = _fused_linear(
        S=S, K=D, N=D, tm=tm, tn=tn, npass=NPASS_MIXER,
        pro_fn=_pro_norm_mod, pro_in=_norm_mod_inputs(h, norm_g, sc, sh, tm),
        w_in=[(w_pw1, _w_spec(D, tn)), (w_pw1, _w_spec(D, tn, nb))],
        epi_fn=_epi_glu,
        epi_in=[(b_pw1.reshape(1, 2 * D), _col_vec_spec(tn)), (b_pw1.reshape(1, 2 * D), _col_vec_spec(tn, nb))])[0]
    halo_blocks = tm // CONV_HALO
    halo_spec = pl.BlockSpec((CONV_HALO, D), lambda i, j: (jnp.maximum(i * halo_blocks - 1, 0), 0))
    return _fused_linear(
        S=S, K=D, N=D, tm=tm, tn=tn, npass=NPASS_MIXER,
        pro_fn=_pro_conv,
        pro_in=[(z, _row_spec(tm, D)), (z, halo_spec),
                (w_dw, pl.BlockSpec((CONV_WIDTH, D), lambda i, j: (0, 0))),
                (b_dw.reshape(1, D), _vec_spec(D)), (ln_g.reshape(1, D), _vec_spec(D)),
                (ln_b.reshape(1, D), _vec_spec(D))],
        pro_scratch=[pltpu.VMEM((tm + CONV_HALO, D), F32), pltpu.VMEM((tm, D), F32)],
        w_in=[(w_pw2, _w_spec(D, tn))],
        epi_fn=_epi_residual_bias,
        epi_in=[(b_pw2.reshape(1, D), _col_vec_spec(tn)), (h, _tile_spec(tm, tn)), (gate, _col_vec_spec(tn))])[0]


def _hgrn_scan_kernel(q_ref, f_ref, i_ref, g_ref, logit_ref, ng_ref, o_ref, st_ref, *, rows, layer, npass):
    np_ = _nparts(npass)

    @pl.when(pl.program_id(1) == 0)
    def _():
        st_ref[...] = jnp.zeros_like(st_ref)

    lg = logit_ref[...]
    e = jnp.exp(lg - jnp.max(lg, axis=0, keepdims=True))
    p = e / jnp.sum(e, axis=0, keepdims=True)
    cs = p[0:1]
    for j in range(1, layer + 1):
        cs = cs + p[j:j + 1]
    lb = cs - p[0:1]
    log_lb = jnp.log(lb)
    log_1m_lb = jnp.log1p(-lb)

    r_i = lax.broadcasted_iota(jnp.int32, (CHUNK, CHUNK), 0)
    c_i = lax.broadcasted_iota(jnp.int32, (CHUNK, CHUNK), 1)
    tri = (r_i >= c_i).astype(BF16)
    col_sub = lax.broadcasted_iota(jnp.int32, (SUB, CHUNK), 1)
    row_mod = lax.broadcasted_iota(jnp.int32, (CHUNK, 1), 0) % SUB
    ng = ng_ref[...]

    def chunk_body(c, carry):
        rws = pl.ds(pl.multiple_of(c * CHUNK, CHUNK), CHUNK)
        q = q_ref[rws, :] * (HG_DK ** -0.5)
        f = f_ref[rws, :]
        v = i_ref[rws, :]
        g = g_ref[rws, :]
        log_sig = jnp.minimum(f, 0.0) - jnp.log1p(jnp.exp(-jnp.abs(f)))
        t2 = log_1m_lb + log_sig
        lf = jnp.maximum(log_lb, t2) + jnp.log1p(jnp.exp(-jnp.abs(log_lb - t2)))
        k = 1.0 - jnp.exp(lf)
        b = None
        for part in _split(lf, 3):
            t = lax.dot_general(tri, part, NN, preferred_element_type=F32)
            b = t if b is None else b + t
        st = st_ref[...]
        o = _mm(_split(q * jnp.exp(b), np_), _split(st, np_), NT, npass)
        pieces = [jnp.zeros((SUB, CHUNK), F32)]
        for blk in range(1, CHUNK // SUB):
            lo = blk * SUB
            ref = b[lo - 1:lo, :]
            q_b = q[lo:lo + SUB, :] * jnp.exp(b[lo:lo + SUB, :] - ref)
            k_b = k * jnp.exp(jnp.minimum(ref - b, 0.0))
            a_b = _mm(_split(q_b, np_), _split(k_b, np_), NT, npass)
            pieces.append(jnp.where(col_sub < lo, a_b, 0.0))
        attn = jnp.concatenate(pieces, axis=0)
        o = o + _mm(_split(attn, np_), _split(v, np_), NN, npass)
        for lag in range(SUB):
            if lag == 0:
                ks, bs, vs = k, b, v
            else:
                ks = pltpu.roll(k, lag, 0)
                bs = pltpu.roll(b, lag, 0)
                vs = pltpu.roll(v, lag, 0)
            valid = row_mod >= lag
            dec = jnp.exp(jnp.where(valid, b - bs, 0.0))
            cl = jnp.sum(q * ks * dec, axis=-1, keepdims=True)
            o = o + jnp.where(valid, cl, 0.0) * vs
        b_last = b[CHUNK - 1:CHUNK, :]
        k_dec = k * jnp.exp(b_last - b)
        st_ref[...] = st * jnp.exp(b_last) + _mm(_split(v, np_), _split(k_dec, np_), TN, npass)
        o_ref[rws, :] = _rms(o, ng) * (g * jax.nn.sigmoid(g))
        return carry

    lax.fori_loop(0, rows // CHUNK, chunk_body, 0)


def _hgrn_mixer(h, norm_g, sc, sh, gate, w_in, lb_logits, layer, hg_norm_g, w_out):
    S, D = h.shape
    tm, tn = _tiles(S, D)
    y4 = _fused_linear(
        S=S, K=D, N=4 * D, tm=tm, tn=tn, npass=NPASS_MIXER,
        pro_fn=_pro_norm_mod, pro_in=_norm_mod_inputs(h, norm_g, sc, sh, tm),
        w_in=[(w_in, _w_spec(D, tn))], epi_fn=_epi_plain, epi_in=[])[0]
    rows = min(512, S)
    depth = lb_logits.shape[0]
    nh = D // HG_DK

    def part_spec(part):
        return pl.BlockSpec((rows, HG_DK), lambda hd, r: (r, part * nh + hd))

    og = pl.pallas_call(
        functools.partial(_hgrn_scan_kernel, rows=rows, layer=layer, npass=NPASS_MIXER),
        grid=(nh, S // rows),
        in_specs=[part_spec(0), part_spec(1), part_spec(2), part_spec(3),
                  pl.BlockSpec((depth, HG_DK), lambda hd, r: (0, hd)),
                  pl.BlockSpec((1, HG_DK), lambda hd, r: (0, 0))],
        out_specs=pl.BlockSpec((rows, HG_DK), lambda hd, r: (r, hd)),
        out_shape=jax.ShapeDtypeStruct((S, D), F32),
        scratch_shapes=[pltpu.VMEM((HG_DK, HG_DK), F32)],
        compiler_params=_params(("arbitrary", "arbitrary")),
    )(y4, y4, y4, y4, lb_logits, hg_norm_g.reshape(1, HG_DK))
    return _fused_linear(
        S=S, K=D, N=D, tm=tm, tn=tn, npass=NPASS_MIXER,
        pro_fn=_pro_identity, pro_in=[(og, _row_spec(tm, D))],
        w_in=[(w_out, _w_spec(D, tn))], epi_fn=_epi_residual,
        epi_in=[(h, _tile_spec(tm, tn)), (gate, _col_vec_spec(tn))])[0]


HEAD_PAD = 2 * LANES
C_PAD = Q_LORA + KV_LORA + LANES


def _swap_rope_halves(x):
    half = QK_ROPE // 2
    lane = lax.broadcasted_iota(jnp.int32, x.shape, 1)
    return jnp.where(lane < half, pltpu.roll(x, LANES - half, 1), pltpu.roll(x, half, 1))


def _mla_proj_kernel(*refs, npass, scale):
    np_ = _nparts(npass)
    c_ref, pos_ref, invf_ref, sgn_ref, qg_ref, kvg_ref, wq_ref, wkv_ref = refs[:8]
    outs = refs[8:8 + 3 * np_]
    q_out, k_out, v_out = outs[:np_], outs[np_:2 * np_], outs[2 * np_:]
    scr = refs[8 + 3 * np_:]
    cq_scr, ckv_scr = scr[:np_], scr[np_:2 * np_]
    cos_scr, sin_scr, kr_scr = scr[2 * np_:]

    @pl.when(pl.program_id(1) == 0)
    def _():
        c = c_ref[...]
        for s, p in zip(cq_scr, _split(_rms(c[:, :Q_LORA], qg_ref[...]), np_)):
            s[...] = p
        for s, p in zip(ckv_scr, _split(_rms(c[:, Q_LORA:Q_LORA + KV_LORA], kvg_ref[...]), np_)):
            s[...] = p
        ang = pos_ref[...].astype(F32) * invf_ref[...]
        cos = jnp.cos(ang)
        sin = jnp.sin(ang) * sgn_ref[...]
        cos_scr[...] = cos
        sin_scr[...] = sin
        kr = c[:, Q_LORA + KV_LORA:]
        kr_scr[...] = kr * cos + _swap_rope_halves(kr) * sin

    cq = tuple(s[...] for s in cq_scr)
    ckv = tuple(s[...] for s in ckv_scr)
    rq = _mm(cq, _split(wq_ref[...], np_), NN, npass)
    q2 = rq[:, LANES:]
    q2 = q2 * cos_scr[...] + _swap_rope_halves(q2) * sin_scr[...]
    rkv = _mm(ckv, _split(wkv_ref[...], np_), NN, npass)
    for o, a, b in zip(q_out, _split(rq[:, :LANES] * scale, np_), _split(q2 * scale, np_)):
        o[:, :LANES] = a
        o[:, LANES:] = b
    for o, a, b in zip(k_out, _split(rkv[:, :LANES], np_), _split(kr_scr[...], np_)):
        o[:, :LANES] = a
        o[:, LANES:] = b
    for o, a in zip(v_out, _split(rkv[:, LANES:], np_)):
        o[...] = a


def _attn_kernel(*refs, npass, tq):
    np_ = _nparts(npass)
    q_refs, k_refs, v_refs = refs[:np_], refs[np_:2 * np_], refs[2 * np_:3 * np_]
    o_ref, m_scr, l_scr, acc_scr = refs[3 * np_:]
    qi = pl.program_id(1)
    ki = pl.program_id(2)

    @pl.when(ki == 0)
    def _():
        m_scr[...] = jnp.full_like(m_scr, -jnp.inf)
        l_scr[...] = jnp.zeros_like(l_scr)
        acc_scr[...] = jnp.zeros_like(acc_scr)

    def step(masked):
        s = _mm(tuple(r[...] for r in q_refs), tuple(r[...] for r in k_refs), NT, npass)
        if masked:
            r_c = lax.broadcasted_iota(jnp.int32, s.shape, 0) // CHUNK
            c_c = lax.broadcasted_iota(jnp.int32, s.shape, 1) // CHUNK
            s = jnp.where(c_c <= r_c, s, -jnp.inf)
        m_prev = m_scr[...]
        m_new = jnp.maximum(m_prev, jnp.max(s, axis=-1, keepdims=True))
        alpha = jnp.exp(m_prev - m_new)
        p = jnp.exp(s - m_new)
        l_scr[...] = alpha * l_scr[...] + jnp.sum(p, axis=-1, keepdims=True)
        acc_scr[...] = alpha * acc_scr[...] + _mm(_split(p, np_), tuple(r[...] for r in v_refs), NN, npass)
        m_scr[...] = m_new

    @pl.when(ki < qi)
    def _():
        step(False)

    @pl.when(ki == qi)
    def _():
        step(True)
        o_ref[...] = acc_scr[...] / l_scr[...]


def _mla_mixer(h, pos, norm_g, sc, sh, gate, w_in, q_norm_g, kv_norm_g, w_uq, w_ukv, w_o):
    S, D = h.shape
    tm, tn = _tiles(S, D)
    npass = NPASS_MIXER
    np_ = _nparts(npass)
    nh = MLA_HEADS
    w_in_p = jnp.pad(w_in, ((0, 0), (0, C_PAD - w_in.shape[1])))
    w_uq_p = jnp.pad(w_uq.reshape(Q_LORA, nh, QK_NOPE + QK_ROPE),
                     ((0, 0), (0, 0), (0, HEAD_PAD - QK_NOPE - QK_ROPE))).reshape(Q_LORA, nh * HEAD_PAD)
    c_lat = _fused_linear(
        S=S, K=D, N=C_PAD, tm=tm, tn=C_PAD, npass=npass,
        pro_fn=_pro_norm_mod, pro_in=_norm_mod_inputs(h, norm_g, sc, sh, tm),
        w_in=[(w_in_p, _w_spec(D, C_PAD))], epi_fn=_epi_plain, epi_in=[])[0]

    inv_freq = ROPE_THETA ** (-jnp.arange(0, QK_ROPE, 2, dtype=F32) / QK_ROPE)
    zeros = jnp.zeros((LANES - QK_ROPE,), F32)
    invf = jnp.concatenate([inv_freq, inv_freq, zeros]).reshape(1, LANES)
    half = QK_ROPE // 2
    sgn = jnp.concatenate([-jnp.ones((half,), F32), jnp.ones((half,), F32), zeros]).reshape(1, LANES)
    scale = (QK_NOPE + QK_ROPE) ** -0.5

    qkv = pl.pallas_call(
        functools.partial(_mla_proj_kernel, npass=npass, scale=scale),
        grid=(S // tm, nh),
        in_specs=[pl.BlockSpec((tm, C_PAD), lambda i, j: (i, 0)),
                  pl.BlockSpec((tm, 1), lambda i, j: (i, 0)),
                  pl.BlockSpec((1, LANES), lambda i, j: (0, 0)),
                  pl.BlockSpec((1, LANES), lambda i, j: (0, 0)),
                  pl.BlockSpec((1, Q_LORA), lambda i, j: (0, 0)),
                  pl.BlockSpec((1, KV_LORA), lambda i, j: (0, 0)),
                  pl.BlockSpec((Q_LORA, HEAD_PAD), lambda i, j: (0, j)),
                  pl.BlockSpec((KV_LORA, QK_NOPE + V_HEAD), lambda i, j: (0, j))],
        out_specs=([pl.BlockSpec((tm, HEAD_PAD), lambda i, j: (i, j))] * (2 * np_)
                   + [pl.BlockSpec((tm, V_HEAD), lambda i, j: (i, j))] * np_),
        out_shape=([jax.ShapeDtypeStruct((S, nh * HEAD_PAD), BF16)] * (2 * np_)
                   + [jax.ShapeDtypeStruct((S, nh * V_HEAD), BF16)] * np_),
        scratch_shapes=([pltpu.VMEM((tm, Q_LORA), BF16)] * np_ + [pltpu.VMEM((tm, KV_LORA), BF16)] * np_
                        + [pltpu.VMEM((tm, LANES), F32)] * 3),
        compiler_params=_params(("arbitrary", "arbitrary")),
    )(c_lat, pos.reshape(S, 1), invf, sgn, q_norm_g.reshape(1, Q_LORA), kv_norm_g.reshape(1, KV_LORA),
      w_uq_p, w_ukv)
    q_parts, k_parts, v_parts = qkv[:np_], qkv[np_:2 * np_], qkv[2 * np_:]

    tq = min(512, S)
    o = pl.pallas_call(
        functools.partial(_attn_kernel, npass=npass, tq=tq),
        grid=(nh, S // tq, S // tq),
        in_specs=([pl.BlockSpec((tq, HEAD_PAD), lambda hd, qi, ki: (qi, hd))] * np_
                  + [pl.BlockSpec((tq, HEAD_PAD), lambda hd, qi, ki: (jnp.minimum(ki, qi), hd))] * np_
                  + [pl.BlockSpec((tq, V_HEAD), lambda hd, qi, ki: (jnp.minimum(ki, qi), hd))] * np_),
        out_specs=pl.BlockSpec((tq, V_HEAD), lambda hd, qi, ki: (qi, hd)),
        out_shape=jax.ShapeDtypeStruct((S, nh * V_HEAD), F32),
        scratch_shapes=[pltpu.VMEM((tq, 1), F32), pltpu.VMEM((tq, 1), F32), pltpu.VMEM((tq, V_HEAD), F32)],
        compiler_params=_params(("arbitrary", "arbitrary", "arbitrary")),
    )(*q_parts, *k_parts, *v_parts)

    return _fused_linear(
        S=S, K=nh * V_HEAD, N=D, tm=tm, tn=tn, npass=npass,
        pro_fn=_pro_identity, pro_in=[(o, _row_spec(tm, nh * V_HEAD))],
        w_in=[(w_o, _w_spec(nh * V_HEAD, tn))], epi_fn=_epi_residual,
        epi_in=[(h, _tile_spec(tm, tn)), (gate, _col_vec_spec(tn))])[0]


NOT_RANKED = 99.0
PAIR_ROWS = 8


def _top16(s, row_id):
    x = s
    rank = jnp.full(s.shape, NOT_RANKED, F32)
    rank_id = lax.broadcasted_iota(jnp.int32, (PEER_TOPK, s.shape[1]), 0)
    stack = jnp.zeros((PEER_TOPK, s.shape[1]), F32)
    vals = []
    for a in range(PEER_TOPK):
        m = jnp.max(x, axis=0, keepdims=True)
        first = jnp.min(jnp.where(x == m, row_id, 1e9), axis=0, keepdims=True)
        hit = row_id == first
        rank = jnp.where(hit, float(a), rank)
        x = jnp.where(hit, -jnp.inf, x)
        stack = jnp.where(rank_id == a, m, stack)
        vals.append(m)
    return vals, stack, rank


def _peer_route_kernel(q_ref, keys_ref, rank2_ref, e2_ref, nrow_ref, crow_ref, *, npass):
    np_ = _nparts(npass)
    tmr = q_ref.shape[0]
    row_id = lax.broadcasted_iota(jnp.int32, (N_KEYS, tmr), 0).astype(F32)
    flat_rows = ([float(b) for b in range(PEER_TOPK)]
                 + [float(a * PEER_TOPK + b) for a in range(1, PAIR_ROWS) for b in range(PAIR_ROWS)]
                 + [float(a * PEER_TOPK) for a in range(PAIR_ROWS, PEER_TOPK)])
    n_cand = len(flat_rows)
    cand_row = lax.broadcasted_iota(jnp.int32, (n_cand, 1), 0)
    flat_id = jnp.zeros((n_cand, 1), F32)
    for r, fv in enumerate(flat_rows):
        flat_id = jnp.where(cand_row == r, fv, flat_id)

    for hd in range(PEER_HEADS):
        halves = []
        for half in range(2):
            col = (2 * hd + half) * N_KEYS
            qh = q_ref[:, col:col + N_KEYS]
            s = _mm(_split(keys_ref[hd, half], np_), _split(qh, np_), NT, npass)
            halves.append((s,) + _top16(s, row_id))
        (s1, v1, t1, rank1), (s2, v2, t2, rank2) = halves
        pieces = [v1[0] + t2]
        for a in range(1, PAIR_ROWS):
            pieces.append(v1[a] + t2[0:PAIR_ROWS])
        pieces.append(t1[PAIR_ROWS:] + v2[0])
        cand = jnp.concatenate(pieces, axis=0)
        x = cand
        for _ in range(PEER_TOPK):
            m = jnp.max(x, axis=0, keepdims=True)
            first = jnp.min(jnp.where(x == m, flat_id, 1e9), axis=0, keepdims=True)
            x = jnp.where(flat_id == first, -jnp.inf, x)
        sel = x == -jnp.inf
        z = jnp.sum(jnp.where(sel, jnp.exp(cand - (v1[0] + v2[0])), 0.0), axis=0, keepdims=True)
        self_f = sel.astype(F32)
        nsel = [jnp.sum(self_f[0:PEER_TOPK], axis=0, keepdims=True)]
        for a in range(1, PAIR_ROWS):
            lo = PEER_TOPK + (a - 1) * PAIR_ROWS
            nsel.append(jnp.sum(self_f[lo:lo + PAIR_ROWS], axis=0, keepdims=True))
        tail = PEER_TOPK + (PAIR_ROWS - 1) * PAIR_ROWS
        for a in range(PAIR_ROWS, PEER_TOPK):
            nsel.append(self_f[tail + a - PAIR_ROWS:tail + a - PAIR_ROWS + 1])
        nrow = jnp.zeros((N_KEYS, tmr), F32)
        for a in range(PEER_TOPK):
            nrow = jnp.where(rank1 == float(a), nsel[a], nrow)
        rank2_ref[hd] = rank2
        e2_ref[hd] = jnp.exp(s2 - v2[0])
        nrow_ref[hd] = nrow
        crow_ref[hd] = jnp.exp(s1 - v1[0]) / z


PEER_EXPERT_TILE = 1024


def _peer_dense_kernel(u_ref, eu_ref, ev_ref, rank2_ref, e2_ref, nrow_ref, crow_ref, h_ref, g_ref,
                       o_ref, w_scr):
    e = pl.program_id(1)

    @pl.when(e == 0)
    def _():
        o_ref[...] = jnp.zeros_like(o_ref)

    u = u_ref[...]
    keys_per_sub = MXU_DIM // N_KEYS
    for sub in range(PEER_EXPERT_TILE // MXU_DIM):
        a = lax.dot_general(u, eu_ref[sub * MXU_DIM:(sub + 1) * MXU_DIM, :], NT,
                            preferred_element_type=F32)
        gts = []
        for ii in range(keys_per_sub):
            i1 = sub * keys_per_sub + ii
            acc = None
            for hd in range(PEER_HEADS):
                nr = nrow_ref[hd, i1:i1 + 1, :]
                cr = crow_ref[hd, i1:i1 + 1, :]
                t = jnp.where(rank2_ref[hd] < nr, e2_ref[hd] * cr, 0.0)
                acc = t if acc is None else acc + t
            gts.append(acc)
        gate = jnp.concatenate(gts, axis=0).T
        w_scr[:, sub * MXU_DIM:(sub + 1) * MXU_DIM] = (jax.nn.gelu(a, approximate=True) * gate).astype(BF16)
    o_ref[...] += lax.dot_general(w_scr[...], ev_ref[...], NN, preferred_element_type=F32)

    @pl.when(e == pl.num_programs(1) - 1)
    def _():
        o_ref[...] = h_ref[...] + g_ref[...] * o_ref[...]


def _peer_mixer(h, norm_g, sc, sh, gate, w_q, sub_keys, eu_bf16, ev_bf16):
    S, D = h.shape
    tm, tn = _tiles(S, D)
    n_exp = eu_bf16.shape[0]
    q, u_bf16 = _fused_linear(
        S=S, K=D, N=PEER_HEADS * 2 * N_KEYS, tm=tm, tn=tn, npass=NPASS_ROUTE,
        pro_fn=_pro_norm_mod, pro_in=_norm_mod_inputs(h, norm_g, sc, sh, tm),
        w_in=[(w_q, _w_spec(D, tn))], epi_fn=_epi_plain, epi_in=[], emit_hi=True)

    gate_shape = jax.ShapeDtypeStruct((PEER_HEADS, N_KEYS, S), F32)
    gate_spec = pl.BlockSpec((PEER_HEADS, N_KEYS, tm), lambda i: (0, 0, i))
    rank2, e2, nrow, crow = pl.pallas_call(
        functools.partial(_peer_route_kernel, npass=NPASS_ROUTE),
        grid=(S // tm,),
        in_specs=[pl.BlockSpec((tm, PEER_HEADS * 2 * N_KEYS), lambda i: (i, 0)),
                  pl.BlockSpec(sub_keys.shape, lambda i: (0, 0, 0, 0))],
        out_specs=[gate_spec] * 4,
        out_shape=[gate_shape] * 4,
        compiler_params=_params(("arbitrary",)),
    )(q, sub_keys)

    keys_per_tile = PEER_EXPERT_TILE // N_KEYS
    res_spec = pl.BlockSpec((PEER_HEADS, N_KEYS, tm), lambda i, e: (0, 0, i))
    row_spec = pl.BlockSpec((PEER_HEADS, keys_per_tile, tm), lambda i, e: (0, e, i))
    return pl.pallas_call(
        _peer_dense_kernel,
        grid=(S // tm, n_exp // PEER_EXPERT_TILE),
        in_specs=[pl.BlockSpec((tm, D), lambda i, e: (i, 0)),
                  pl.BlockSpec((PEER_EXPERT_TILE, D), lambda i, e: (e, 0)),
                  pl.BlockSpec((PEER_EXPERT_TILE, D), lambda i, e: (e, 0)),
                  res_spec, res_spec, row_spec, row_spec,
                  pl.BlockSpec((tm, D), lambda i, e: (i, 0)),
                  pl.BlockSpec((1, D), lambda i, e: (0, 0))],
        out_specs=pl.BlockSpec((tm, D), lambda i, e: (i, 0)),
        out_shape=jax.ShapeDtypeStruct((S, D), F32),
        scratch_shapes=[pltpu.VMEM((tm, PEER_EXPERT_TILE), BF16)],
        compiler_params=_params(("arbitrary", "arbitrary")),
    )(u_bf16, eu_bf16, ev_bf16, rank2, e2, nrow, crow, h, gate)


def _final_norm_kernel(x_ref, g_ref, o_ref):
    o_ref[...] = _rms(x_ref[...], g_ref[...])


def _final_norm(h, g):
    S, D = h.shape
    tm = min(512, S)
    return pl.pallas_call(
        _final_norm_kernel,
        grid=(S // tm,),
        in_specs=[pl.BlockSpec((tm, D), lambda i: (i, 0)), pl.BlockSpec((1, D), lambda i: (0, 0))],
        out_specs=pl.BlockSpec((tm, D), lambda i: (i, 0)),
        out_shape=jax.ShapeDtypeStruct((S, D), F32),
        compiler_params=_params(("arbitrary",)),
    )(h, g.reshape(1, D))


def kernel(x, c, positions, ada_w, ada_b, norm_g, conv_w_pw1, conv_b_pw1, conv_w_dw, conv_b_dw, conv_ln_g, conv_ln_b, conv_w_pw2, conv_b_pw2, hg_w_in, hg_lb_logits, hg_norm_g, hg_w_out, mla_w_in, mla_q_norm_g, mla_kv_norm_g, mla_w_uq, mla_w_ukv, mla_w_o, peer_w_q, peer_sub_keys, peer_u, peer_v, final_g):
    B, S, D = x.shape
    assert B == 1
    depth = ada_w.shape[0]
    mod = _ada_mod(c, ada_w, ada_b)
    eu = peer_u.astype(BF16)
    ev = peer_v.astype(BF16)
    h = x.reshape(S, D)
    for i in range(depth):
        sh1, sc1, g1, sh2, sc2, g2 = [mod[i, :, k * D:(k + 1) * D] for k in range(6)]
        kind, slot = i % 3, i // 3
        if kind == 0:
            h = _conv_mixer(h, norm_g[i, 0], sc1, sh1, g1, conv_w_pw1[slot], conv_b_pw1[slot], conv_w_dw[slot],
                            conv_b_dw[slot], conv_ln_g[slot], conv_ln_b[slot], conv_w_pw2[slot], conv_b_pw2[slot])
        elif kind == 1:
            h = _hgrn_mixer(h, norm_g[i, 0], sc1, sh1, g1, hg_w_in[slot], hg_lb_logits, i, hg_norm_g[slot],
                            hg_w_out[slot])
        else:
            h = _mla_mixer(h, positions, norm_g[i, 0], sc1, sh1, g1, mla_w_in[slot], mla_q_norm_g[slot],
                           mla_kv_norm_g[slot], mla_w_uq[slot], mla_w_ukv[slot], mla_w_o[slot])
        h = _peer_mixer(h, norm_g[i, 1], sc2, sh2, g2, peer_w_q[i], peer_sub_keys[i], eu[i], ev[i])
    return _final_norm(h, final_g).reshape(B, S, D)
```

```python
import functools
import math

import jax
import jax.numpy as jnp
from jax import lax
from jax.experimental import pallas as pl
from jax.experimental.pallas import tpu as pltpu

F32 = jnp.float32
BF16 = jnp.bfloat16

CHUNK = 64
EPS = 1e-6
CONV_WIDTH = 31
HG_HEADS = 16
HG_DK = 128
MLA_HEADS = 16
Q_LORA = 512
KV_LORA = 512
QK_NOPE = 128
QK_ROPE = 64
V_HEAD = 128
ROPE_THETA = 10000.0
PEER_HEADS = 8
N_KEYS = 128
PEER_TOPK = 16

LANES = 128
SUBLANES = 8
MXU_DIM = 256
VMEM_LIMIT_BYTES = 56 * 1024 * 1024

NPASS_MIXER = 1
NPASS_PEER_Q = 1
NPASS_ROUTE = 3
SUB = 16

NN = (((1,), (0,)), ((), ()))
NT = (((1,), (1,)), ((), ()))
TN = (((0,), (0,)), ((), ()))


def _params(sem):
    return pltpu.CompilerParams(dimension_semantics=sem, vmem_limit_bytes=VMEM_LIMIT_BYTES)


def _split(x, n):
    hi = x.astype(BF16)
    if n == 1:
        return (hi,)
    r = x - hi.astype(F32)
    mid = r.astype(BF16)
    if n == 2:
        return (hi, mid)
    lo = (r - mid.astype(F32)).astype(BF16)
    return (hi, mid, lo)


def _nparts(npass):
    return 1 if npass == 1 else 2


def _mm(a_parts, b_parts, dn, npass):
    order = 0 if npass == 1 else 1
    out = None
    for i, a in enumerate(a_parts):
        for j, b in enumerate(b_parts):
            if i + j > order:
                continue
            t = lax.dot_general(a, b, dn, preferred_element_type=F32)
            out = t if out is None else out + t
    return out


def _rms(x, g):
    return x * lax.rsqrt(jnp.mean(x * x, axis=-1, keepdims=True) + EPS) * g


def _norm_mod(x, g, sc, sh):
    return _rms(x, g) * (1.0 + sc) + sh


def _mod_kernel(c_ref, w_ref, b_ref, o_ref):
    c = c_ref[...]
    cond = c * jax.nn.sigmoid(c)
    o_ref[0] = jnp.sum(w_ref[0] * cond, axis=0, keepdims=True) + b_ref[0]


def _ada_mod(c, ada_w, ada_b):
    depth, d, n = ada_w.shape
    tn = 1536 if n % 1536 == 0 else n
    return pl.pallas_call(
        _mod_kernel,
        grid=(depth, n // tn),
        in_specs=[pl.BlockSpec((d, 1), lambda i, j: (0, 0)),
                  pl.BlockSpec((1, d, tn), lambda i, j: (i, 0, j)),
                  pl.BlockSpec((1, 1, tn), lambda i, j: (i, 0, j))],
        out_specs=pl.BlockSpec((1, 1, tn), lambda i, j: (i, 0, j)),
        out_shape=jax.ShapeDtypeStruct((depth, 1, n), F32),
        compiler_params=_params(("arbitrary", "arbitrary")),
    )(c.reshape(d, 1), ada_w, ada_b.reshape(depth, 1, n))


def _linear_kernel(*refs, n_pro, n_w, n_epi, n_out, n_pscr, npass, pro_fn, epi_fn, emit_hi):
    np_ = _nparts(npass)
    pro_refs = refs[:n_pro]
    w_refs = refs[n_pro:n_pro + n_w]
    epi_refs = refs[n_pro + n_w:n_pro + n_w + n_epi]
    out_refs = refs[n_pro + n_w + n_epi:n_pro + n_w + n_epi + n_out]
    rest = refs[n_pro + n_w + n_epi + n_out:]
    hi_ref = None
    if emit_hi:
        hi_ref, rest = rest[0], rest[1:]
    u_scr = rest[:np_]
    p_scr = rest[np_:np_ + n_pscr]

    row_tile = pl.program_id(0)

    @pl.when(pl.program_id(1) == 0)
    def _():
        u = pro_fn(row_tile, *pro_refs, *p_scr)
        parts = _split(u, np_)
        for s, p in zip(u_scr, parts):
            s[...] = p
        if emit_hi:
            hi_ref[...] = parts[0]

    a_parts = tuple(s[...] for s in u_scr)
    accs = [_mm(a_parts, _split(w[...], np_), NN, npass) for w in w_refs]
    outs = epi_fn(accs, *epi_refs)
    for o_ref, o in zip(out_refs, outs):
        o_ref[...] = o


def _fused_linear(*, S, K, N, tm, tn, npass, pro_fn, pro_in, w_in, epi_fn, epi_in,
                  n_out=1, emit_hi=False, pro_scratch=()):
    np_ = _nparts(npass)
    arrays = [a for a, _ in pro_in] + [a for a, _ in w_in] + [a for a, _ in epi_in]
    specs = [s for _, s in pro_in] + [s for _, s in w_in] + [s for _, s in epi_in]
    out_shape = [jax.ShapeDtypeStruct((S, N), F32)] * n_out
    out_specs = [pl.BlockSpec((tm, tn), lambda i, j: (i, j))] * n_out
    if emit_hi:
        out_shape = out_shape + [jax.ShapeDtypeStruct((S, K), BF16)]
        out_specs = out_specs + [pl.BlockSpec((tm, K), lambda i, j: (i, 0))]
    kern = functools.partial(
        _linear_kernel, n_pro=len(pro_in), n_w=len(w_in), n_epi=len(epi_in), n_out=n_out,
        n_pscr=len(pro_scratch), npass=npass, pro_fn=pro_fn, epi_fn=epi_fn, emit_hi=emit_hi)
    return pl.pallas_call(
        kern,
        grid=(S // tm, N // tn),
        in_specs=specs,
        out_specs=out_specs,
        out_shape=out_shape,
        scratch_shapes=[pltpu.VMEM((tm, K), BF16)] * np_ + list(pro_scratch),
        compiler_params=_params(("arbitrary", "arbitrary")),
    )(*arrays)


def _row_spec(tm, k):
    return pl.BlockSpec((tm, k), lambda i, j: (i, 0))


def _vec_spec(k):
    return pl.BlockSpec((1, k), lambda i, j: (0, 0))


def _col_vec_spec(tn, off=0):
    return pl.BlockSpec((1, tn), lambda i, j: (0, j + off))


def _w_spec(k, tn, off=0):
    return pl.BlockSpec((k, tn), lambda i, j: (0, j + off))


def _tile_spec(tm, tn):
    return pl.BlockSpec((tm, tn), lambda i, j: (i, j))


def _pro_norm_mod(row_tile, x_ref, g_ref, sc_ref, sh_ref):
    return _norm_mod(x_ref[...], g_ref[...], sc_ref[...], sh_ref[...])


def _pro_identity(row_tile, x_ref):
    return x_ref[...]


def _epi_plain(accs):
    return (accs[0],)


def _epi_residual(accs, h_ref, g_ref):
    return (h_ref[...] + g_ref[...] * accs[0],)


def _epi_residual_bias(accs, b_ref, h_ref, g_ref):
    return (h_ref[...] + g_ref[...] * (accs[0] + b_ref[...]),)


def _epi_glu(accs, b1_ref, b2_ref):
    return ((accs[0] + b1_ref[...]) * jax.nn.sigmoid(accs[1] + b2_ref[...]),)


def _tiles(S, N):
    tm = min(512, S)
    tn = 512 if N % 512 == 0 else N
    return tm, tn


def _norm_mod_inputs(h, g, sc, sh, tm):
    d = h.shape[1]
    return [(h, _row_spec(tm, d)), (g.reshape(1, d), _vec_spec(d)), (sc, _vec_spec(d)), (sh, _vec_spec(d))]


CONV_HALO = 32
CONV_ROWS = 64


def _pro_conv(row_tile, z_ref, halo_ref, wdw_ref, bdw_ref, lng_ref, lnb_ref, zbuf, cbuf):
    tm, d = z_ref.shape
    first = row_tile == 0
    zbuf[0:CONV_HALO, :] = jnp.where(first, 0.0, halo_ref[...])
    zbuf[CONV_HALO:, :] = z_ref[...]
    lead = CONV_HALO - (CONV_WIDTH - 1)

    def col_body(cb, carry):
        c0 = pl.multiple_of(cb * LANES, LANES)
        cols = pl.ds(c0, LANES)
        for rc in range(tm // CONV_ROWS):
            acc = jnp.zeros((CONV_ROWS, LANES), F32)
            for w in range(CONV_WIDTH):
                acc = acc + zbuf[pl.ds(rc * CONV_ROWS + lead + w, CONV_ROWS), cols] * wdw_ref[pl.ds(w, 1), cols]
            cbuf[pl.ds(rc * CONV_ROWS, CONV_ROWS), cols] = acc + bdw_ref[:, cols]
        return carry

    lax.fori_loop(0, d // LANES, col_body, 0)
    y = cbuf[...]
    mu = jnp.mean(y, axis=-1, keepdims=True)
    var = jnp.mean(jnp.square(y - mu), axis=-1, keepdims=True)
    y = (y - mu) * lax.rsqrt(var + EPS) * lng_ref[...] + lnb_ref[...]
    return y * jax.nn.sigmoid(y)


def _conv_mixer(h, norm_g, sc, sh, gate, w_pw1, b_pw1, w_dw, b_dw, ln_g, ln_b, w_pw2, b_pw2):
    S, D = h.shape
    tm, tn = _tiles(S, D)
    nb = D // tn
    z = _fused_linear(
        S=S, K=D, N=D, tm=tm, tn=tn, npass=NPASS_MIXER,
        pro_fn=_pro_norm_mod, pro_in=_norm_mod_inputs(h, norm_g, sc, sh, tm),
        w_in=[(w_pw1, _w_spec(D, tn)), (w_pw1, _w_spec(D, tn, nb))],
        epi_fn=_epi_glu,
        epi_in=[(b_pw1.reshape(1, 2 * D), _col_vec_spec(tn)), (b_pw1.reshape(1, 2 * D), _col_vec_spec(tn, nb))])[0]
    halo_blocks = tm // CONV_HALO
    halo_spec = pl.BlockSpec((CONV_HALO, D), lambda i, j: (jnp.maximum(i * halo_blocks - 1, 0), 0))
    return _fused_linear(
        S=S, K=D, N=D, tm=tm, tn=tn, npass=NPASS_MIXER,
        pro_fn=_pro_conv,
        pro_in=[(z, _row_spec(tm, D)), (z, halo_spec),
                (w_dw, pl.BlockSpec((CONV_WIDTH, D), lambda i, j: (0, 0))),
                (b_dw.reshape(1, D), _vec_spec(D)), (ln_g.reshape(1, D), _vec_spec(D)),
                (ln_b.reshape(1, D), _vec_spec(D))],
        pro_scratch=[pltpu.VMEM((tm + CONV_HALO, D), F32), pltpu.VMEM((tm, D), F32)],
        w_in=[(w_pw2, _w_spec(D, tn))],
        epi_fn=_epi_residual_bias,
        epi_in=[(b_pw2.reshape(1, D), _col_vec_spec(tn)), (h, _tile_spec(tm, tn)), (gate, _col_vec_spec(tn))])[0]


def _hgrn_scan_kernel(q_ref, f_ref, i_ref, g_ref, logit_ref, ng_ref, o_ref, st_ref, *, rows, layer, npass):
    np_ = _nparts(npass)

    @pl.when(pl.program_id(1) == 0)
    def _():
        st_ref[...] = jnp.zeros_like(st_ref)

    lg = logit_ref[...]
    e = jnp.exp(lg - jnp.max(lg, axis=0, keepdims=True))
    p = e / jnp.sum(e, axis=0, keepdims=True)
    cs = p[0:1]
    for j in range(1, layer + 1):
        cs = cs + p[j:j + 1]
    lb = cs - p[0:1]
    log_lb = jnp.log(lb)
    log_1m_lb = jnp.log1p(-lb)

    r_i = lax.broadcasted_iota(jnp.int32, (CHUNK, CHUNK), 0)
    c_i = lax.broadcasted_iota(jnp.int32, (CHUNK, CHUNK), 1)
    tri = (r_i >= c_i).astype(BF16)
    col_sub = lax.broadcasted_iota(jnp.int32, (SUB, CHUNK), 1)
    row_mod = lax.broadcasted_iota(jnp.int32, (CHUNK, 1), 0) % SUB
    ng = ng_ref[...]

    def chunk_body(c, carry):
        rws = pl.ds(pl.multiple_of(c * CHUNK, CHUNK), CHUNK)
        q = q_ref[rws, :] * (HG_DK ** -0.5)
        f = f_ref[rws, :]
        v = i_ref[rws, :]
        g = g_ref[rws, :]
        log_sig = jnp.minimum(f, 0.0) - jnp.log1p(jnp.exp(-jnp.abs(f)))
        t2 = log_1m_lb + log_sig
        lf = jnp.maximum(log_lb, t2) + jnp.log1p(jnp.exp(-jnp.abs(log_lb - t2)))
        k = 1.0 - jnp.exp(lf)
        b = None
        for part in _split(lf, 3):
            t = lax.dot_general(tri, part, NN, preferred_element_type=F32)
            b = t if b is None else b + t
        st = st_ref[...]
        o = _mm(_split(q * jnp.exp(b), np_), _split(st, np_), NT, npass)
        pieces = [jnp.zeros((SUB, CHUNK), F32)]
        for blk in range(1, CHUNK // SUB):
            lo = blk * SUB
            ref = b[lo - 1:lo, :]
            q_b = q[lo:lo + SUB, :] * jnp.exp(b[lo:lo + SUB, :] - ref)
            k_b = k * jnp.exp(jnp.minimum(ref - b, 0.0))
            a_b = _mm(_split(q_b, np_), _split(k_b, np_), NT, npass)
            pieces.append(jnp.where(col_sub < lo, a_b, 0.0))
        attn = jnp.concatenate(pieces, axis=0)
        o = o + _mm(_split(attn, np_), _split(v, np_), NN, npass)
        for lag in range(SUB):
            if lag == 0:
                ks, bs, vs = k, b, v
            else:
                ks = pltpu.roll(k, lag, 0)
                bs = pltpu.roll(b, lag, 0)
                vs = pltpu.roll(v, lag, 0)
            valid = row_mod >= lag
            dec = jnp.exp(jnp.where(valid, b - bs, 0.0))
            cl = jnp.sum(q * ks * dec, axis=-1, keepdims=True)
            o = o + jnp.where(valid, cl, 0.0) * vs
        b_last = b[CHUNK - 1:CHUNK, :]
        k_dec = k * jnp.exp(b_last - b)
        st_ref[...] = st * jnp.exp(b_last) + _mm(_split(v, np_), _split(k_dec, np_), TN, npass)
        o_ref[rws, :] = _rms(o, ng) * (g * jax.nn.sigmoid(g))
        return carry

    lax.fori_loop(0, rows // CHUNK, chunk_body, 0)


def _hgrn_mixer(h, norm_g, sc, sh, gate, w_in, lb_logits, layer, hg_norm_g, w_out):
    S, D = h.shape
    tm, tn = _tiles(S, D)
    y4 = _fused_linear(
        S=S, K=D, N=4 * D, tm=tm, tn=tn, npass=NPASS_MIXER,
        pro_fn=_pro_norm_mod, pro_in=_norm_mod_inputs(h, norm_g, sc, sh, tm),
        w_in=[(w_in, _w_spec(D, tn))], epi_fn=_epi_plain, epi_in=[])[0]
    rows = min(512, S)
    depth = lb_logits.shape[0]
    nh = D // HG_DK

    def part_spec(part):
        return pl.BlockSpec((rows, HG_DK), lambda hd, r: (r, part * nh + hd))

    og = pl.pallas_call(
        functools.partial(_hgrn_scan_kernel, rows=rows, layer=layer, npass=NPASS_MIXER),
        grid=(nh, S // rows),
        in_specs=[part_spec(0), part_spec(1), part_spec(2), part_spec(3),
                  pl.BlockSpec((depth, HG_DK), lambda hd, r: (0, hd)),
                  pl.BlockSpec((1, HG_DK), lambda hd, r: (0, 0))],
        out_specs=pl.BlockSpec((rows, HG_DK), lambda hd, r: (r, hd)),
        out_shape=jax.ShapeDtypeStruct((S, D), F32),
        scratch_shapes=[pltpu.VMEM((HG_DK, HG_DK), F32)],
        compiler_params=_params(("arbitrary", "arbitrary")),
    )(y4, y4, y4, y4, lb_logits, hg_norm_g.reshape(1, HG_DK))
    return _fused_linear(
        S=S, K=D, N=D, tm=tm, tn=tn, npass=NPASS_MIXER,
        pro_fn=_pro_identity, pro_in=[(og, _row_spec(tm, D))],
        w_in=[(w_out, _w_spec(D, tn))], epi_fn=_epi_residual,
        epi_in=[(h, _tile_spec(tm, tn)), (gate, _col_vec_spec(tn))])[0]


HEAD_PAD = 2 * LANES
C_PAD = Q_LORA + KV_LORA + LANES


def _swap_rope_halves(x):
    half = QK_ROPE // 2
    lane = lax.broadcasted_iota(jnp.int32, x.shape, 1)
    return jnp.where(lane < half, pltpu.roll(x, LANES - half, 1), pltpu.roll(x, half, 1))


def _mla_proj_kernel(*refs, npass, scale):
    np_ = _nparts(npass)
    c_ref, pos_ref, invf_ref, sgn_ref, qg_ref, kvg_ref, wq_ref, wkv_ref = refs[:8]
    outs = refs[8:8 + 3 * np_]
    q_out, k_out, v_out = outs[:np_], outs[np_:2 * np_], outs[2 * np_:]
    scr = refs[8 + 3 * np_:]
    cq_scr, ckv_scr = scr[:np_], scr[np_:2 * np_]
    cos_scr, sin_scr, kr_scr = scr[2 * np_:]

    @pl.when(pl.program_id(1) == 0)
    def _():
        c = c_ref[...]
        for s, p in zip(cq_scr, _split(_rms(c[:, :Q_LORA], qg_ref[...]), np_)):
            s[...] = p
        for s, p in zip(ckv_scr, _split(_rms(c[:, Q_LORA:Q_LORA + KV_LORA], kvg_ref[...]), np_)):
            s[...] = p
        ang = pos_ref[...].astype(F32) * invf_ref[...]
        cos = jnp.cos(ang)
        sin = jnp.sin(ang) * sgn_ref[...]
        cos_scr[...] = cos
        sin_scr[...] = sin
        kr = c[:, Q_LORA + KV_LORA:]
        kr_scr[...] = kr * cos + _swap_rope_halves(kr) * sin

    cq = tuple(s[...] for s in cq_scr)
    ckv = tuple(s[...] for s in ckv_scr)
    rq = _mm(cq, _split(wq_ref[...], np_), NN, npass)
    q2 = rq[:, LANES:]
    q2 = q2 * cos_scr[...] + _swap_rope_halves(q2) * sin_scr[...]
    rkv = _mm(ckv, _split(wkv_ref[...], np_), NN, npass)
    for o, a, b in zip(q_out, _split(rq[:, :LANES] * scale, np_), _split(q2 * scale, np_)):
        o[:, :LANES] = a
        o[:, LANES:] = b
    for o, a, b in zip(k_out, _split(rkv[:, :LANES], np_), _split(kr_scr[...], np_)):
        o[:, :LANES] = a
        o[:, LANES:] = b
    for o, a in zip(v_out, _split(rkv[:, LANES:], np_)):
        o[...] = a


def _attn_kernel(*refs, npass, tq):
    np_ = _nparts(npass)
    q_refs, k_refs, v_refs = refs[:np_], refs[np_:2 * np_], refs[2 * np_:3 * np_]
    o_ref, m_scr, l_scr, acc_scr = refs[3 * np_:]
    qi = pl.program_id(1)
    ki = pl.program_id(2)

    @pl.when(ki == 0)
    def _():
        m_scr[...] = jnp.full_like(m_scr, -jnp.inf)
        l_scr[...] = jnp.zeros_like(l_scr)
        acc_scr[...] = jnp.zeros_like(acc_scr)

    def step(masked):
        s = _mm(tuple(r[...] for r in q_refs), tuple(r[...] for r in k_refs), NT, npass)
        if masked:
            r_c = lax.broadcasted_iota(jnp.int32, s.shape, 0) // CHUNK
            c_c = lax.broadcasted_iota(jnp.int32, s.shape, 1) // CHUNK
            s = jnp.where(c_c <= r_c, s, -jnp.inf)
        m_prev = m_scr[...]
        m_new = jnp.maximum(m_prev, jnp.max(s, axis=-1, keepdims=True))
        alpha = jnp.exp(m_prev - m_new)
        p = jnp.exp(s - m_new)
        l_scr[...] = alpha * l_scr[...] + jnp.sum(p, axis=-1, keepdims=True)
        acc_scr[...] = alpha * acc_scr[...] + _mm(_split(p, np_), tuple(r[...] for r in v_refs), NN, npass)
        m_scr[...] = m_new

    @pl.when(ki < qi)
    def _():
        step(False)

    @pl.when(ki == qi)
    def _():
        step(True)
        o_ref[...] = acc_scr[...] / l_scr[...]


def _mla_mixer(h, pos, norm_g, sc, sh, gate, w_in, q_norm_g, kv_norm_g, w_uq, w_ukv, w_o):
    S, D = h.shape
    tm, tn = _tiles(S, D)
    npass = NPASS_MIXER
    np_ = _nparts(npass)
    nh = MLA_HEADS
    w_in_p = jnp.pad(w_in, ((0, 0), (0, C_PAD - w_in.shape[1])))
    w_uq_p = jnp.pad(w_uq.reshape(Q_LORA, nh, QK_NOPE + QK_ROPE),
                     ((0, 0), (0, 0), (0, HEAD_PAD - QK_NOPE - QK_ROPE))).reshape(Q_LORA, nh * HEAD_PAD)
    c_lat = _fused_linear(
        S=S, K=D, N=C_PAD, tm=tm, tn=C_PAD, npass=npass,
        pro_fn=_pro_norm_mod, pro_in=_norm_mod_inputs(h, norm_g, sc, sh, tm),
        w_in=[(w_in_p, _w_spec(D, C_PAD))], epi_fn=_epi_plain, epi_in=[])[0]

    inv_freq = ROPE_THETA ** (-jnp.arange(0, QK_ROPE, 2, dtype=F32) / QK_ROPE)
    zeros = jnp.zeros((LANES - QK_ROPE,), F32)
    invf = jnp.concatenate([inv_freq, inv_freq, zeros]).reshape(1, LANES)
    half = QK_ROPE // 2
    sgn = jnp.concatenate([-jnp.ones((half,), F32), jnp.ones((half,), F32), zeros]).reshape(1, LANES)
    scale = (QK_NOPE + QK_ROPE) ** -0.5

    qkv = pl.pallas_call(
        functools.partial(_mla_proj_kernel, npass=npass, scale=scale),
        grid=(S // tm, nh),
        in_specs=[pl.BlockSpec((tm, C_PAD), lambda i, j: (i, 0)),
                  pl.BlockSpec((tm, 1), lambda i, j: (i, 0)),
                  pl.BlockSpec((1, LANES), lambda i, j: (0, 0)),
                  pl.BlockSpec((1, LANES), lambda i, j: (0, 0)),
                  pl.BlockSpec((1, Q_LORA), lambda i, j: (0, 0)),
                  pl.BlockSpec((1, KV_LORA), lambda i, j: (0, 0)),
                  pl.BlockSpec((Q_LORA, HEAD_PAD), lambda i, j: (0, j)),
                  pl.BlockSpec((KV_LORA, QK_NOPE + V_HEAD), lambda i, j: (0, j))],
        out_specs=([pl.BlockSpec((tm, HEAD_PAD), lambda i, j: (i, j))] * (2 * np_)
                   + [pl.BlockSpec((tm, V_HEAD), lambda i, j: (i, j))] * np_),
        out_shape=([jax.ShapeDtypeStruct((S, nh * HEAD_PAD), BF16)] * (2 * np_)
                   + [jax.ShapeDtypeStruct((S, nh * V_HEAD), BF16)] * np_),
        scratch_shapes=([pltpu.VMEM((tm, Q_LORA), BF16)] * np_ + [pltpu.VMEM((tm, KV_LORA), BF16)] * np_
                        + [pltpu.VMEM((tm, LANES), F32)] * 3),
        compiler_params=_params(("arbitrary", "arbitrary")),
    )(c_lat, pos.reshape(S, 1), invf, sgn, q_norm_g.reshape(1, Q_LORA), kv_norm_g.reshape(1, KV_LORA),
      w_uq_p, w_ukv)
    q_parts, k_parts, v_parts = qkv[:np_], qkv[np_:2 * np_], qkv[2 * np_:]

    tq = min(512, S)
    o = pl.pallas_call(
        functools.partial(_attn_kernel, npass=npass, tq=tq),
        grid=(nh, S // tq, S // tq),
        in_specs=([pl.BlockSpec((tq, HEAD_PAD), lambda hd, qi, ki: (qi, hd))] * np_
                  + [pl.BlockSpec((tq, HEAD_PAD), lambda hd, qi, ki: (jnp.minimum(ki, qi), hd))] * np_
                  + [pl.BlockSpec((tq, V_HEAD), lambda hd, qi, ki: (jnp.minimum(ki, qi), hd))] * np_),
        out_specs=pl.BlockSpec((tq, V_HEAD), lambda hd, qi, ki: (qi, hd)),
        out_shape=jax.ShapeDtypeStruct((S, nh * V_HEAD), F32),
        scratch_shapes=[pltpu.VMEM((tq, 1), F32), pltpu.VMEM((tq, 1), F32), pltpu.VMEM((tq, V_HEAD), F32)],
        compiler_params=_params(("arbitrary", "arbitrary", "arbitrary")),
    )(*q_parts, *k_parts, *v_parts)

    return _fused_linear(
        S=S, K=nh * V_HEAD, N=D, tm=tm, tn=tn, npass=npass,
        pro_fn=_pro_identity, pro_in=[(o, _row_spec(tm, nh * V_HEAD))],
        w_in=[(w_o, _w_spec(nh * V_HEAD, tn))], epi_fn=_epi_residual,
        epi_in=[(h, _tile_spec(tm, tn)), (gate, _col_vec_spec(tn))])[0]


NOT_RANKED = 99.0
PAIR_ROWS = 8


def _top16(s, row_id):
    x = s
    rank = jnp.full(s.shape, NOT_RANKED, F32)
    rank_id = lax.broadcasted_iota(jnp.int32, (PEER_TOPK, s.shape[1]), 0)
    stack = jnp.zeros((PEER_TOPK, s.shape[1]), F32)
    vals = []
    for a in range(PEER_TOPK):
        m = jnp.max(x, axis=0, keepdims=True)
        first = jnp.min(jnp.where(x == m, row_id, 1e9), axis=0, keepdims=True)
        hit = row_id == first
        rank = jnp.where(hit, float(a), rank)
        x = jnp.where(hit, -jnp.inf, x)
        stack = jnp.where(rank_id == a, m, stack)
        vals.append(m)
    return vals, stack, rank


def _peer_route_kernel(q_ref, keys_ref, rank2_ref, e2_ref, nrow_ref, crow_ref, *, npass):
    np_ = _nparts(npass)
    tmr = q_ref.shape[0]
    row_id = lax.broadcasted_iota(jnp.int32, (N_KEYS, tmr), 0).astype(F32)
    flat_rows = ([float(b) for b in range(PEER_TOPK)]
                 + [float(a * PEER_TOPK + b) for a in range(1, PAIR_ROWS) for b in range(PAIR_ROWS)]
                 + [float(a * PEER_TOPK) for a in range(PAIR_ROWS, PEER_TOPK)])
    n_cand = len(flat_rows)
    cand_row = lax.broadcasted_iota(jnp.int32, (n_cand, 1), 0)
    flat_id = jnp.zeros((n_cand, 1), F32)
    for r, fv in enumerate(flat_rows):
        flat_id = jnp.where(cand_row == r, fv, flat_id)

    for hd in range(PEER_HEADS):
        halves = []
        for half in range(2):
            col = (2 * hd + half) * N_KEYS
            qh = q_ref[:, col:col + N_KEYS]
            s = _mm(_split(keys_ref[hd, half], np_), _split(qh, np_), NT, npass)
            halves.append((s,) + _top16(s, row_id))
        (s1, v1, t1, rank1), (s2, v2, t2, rank2) = halves
        pieces = [v1[0] + t2]
        for a in range(1, PAIR_ROWS):
            pieces.append(v1[a] + t2[0:PAIR_ROWS])
        pieces.append(t1[PAIR_ROWS:] + v2[0])
        cand = jnp.concatenate(pieces, axis=0)
        x = cand
        for _ in range(PEER_TOPK):
            m = jnp.max(x, axis=0, keepdims=True)
            first = jnp.min(jnp.where(x == m, flat_id, 1e9), axis=0, keepdims=True)
            x = jnp.where(flat_id == first, -jnp.inf, x)
        sel = x == -jnp.inf
        z = jnp.sum(jnp.where(sel, jnp.exp(cand - (v1[0] + v2[0])), 0.0), axis=0, keepdims=True)
        self_f = sel.astype(F32)
        nsel = [jnp.sum(self_f[0:PEER_TOPK], axis=0, keepdims=True)]
        for a in range(1, PAIR_ROWS):
            lo = PEER_TOPK + (a - 1) * PAIR_ROWS
            nsel.append(jnp.sum(self_f[lo:lo + PAIR_ROWS], axis=0, keepdims=True))
        tail = PEER_TOPK + (PAIR_ROWS - 1) * PAIR_ROWS
        for a in range(PAIR_ROWS, PEER_TOPK):
            nsel.append(self_f[tail + a - PAIR_ROWS:tail + a - PAIR_ROWS + 1])
        nrow = jnp.zeros((N_KEYS, tmr), F32)
        for a in range(PEER_TOPK):
            nrow = jnp.where(rank1 == float(a), nsel[a], nrow)
        rank2_ref[hd] = rank2
        e2_ref[hd] = jnp.exp(s2 - v2[0])
        nrow_ref[hd] = nrow
        crow_ref[hd] = jnp.exp(s1 - v1[0]) / z


PEER_EXPERT_TILE = 1024


def _peer_dense_kernel(u_ref, eu_ref, ev_ref, rank2_ref, e2_ref, nrow_ref, crow_ref, h_ref, g_ref,
                       o_ref, w_scr):
    e = pl.program_id(1)

    @pl.when(e == 0)
    def _():
        o_ref[...] = jnp.zeros_like(o_ref)

    u = u_ref[...]
    keys_per_sub = MXU_DIM // N_KEYS
    for sub in range(PEER_EXPERT_TILE // MXU_DIM):
        a = lax.dot_general(u, eu_ref[sub * MXU_DIM:(sub + 1) * MXU_DIM, :], NT,
                            preferred_element_type=F32)
        gts = []
        for ii in range(keys_per_sub):
            i1 = sub * keys_per_sub + ii
            acc = None
            for hd in range(PEER_HEADS):
                nr = nrow_ref[hd, i1:i1 + 1, :]
                cr = crow_ref[hd, i1:i1 + 1, :]
                t = jnp.where(rank2_ref[hd] < nr, e2_ref[hd] * cr, 0.0)
                acc = t if acc is None else acc + t
            gts.append(acc)
        gate = jnp.concatenate(gts, axis=0).T
        w_scr[:, sub * MXU_DIM:(sub + 1) * MXU_DIM] = (jax.nn.gelu(a, approximate=True) * gate).astype(BF16)
    o_ref[...] += lax.dot_general(w_scr[...], ev_ref[...], NN, preferred_element_type=F32)

    @pl.when(e == pl.num_programs(1) - 1)
    def _():
        o_ref[...] = h_ref[...] + g_ref[...] * o_ref[...]


def _peer_mixer(h, norm_g, sc, sh, gate, w_q, sub_keys, eu_bf16, ev_bf16, layer):
    S, D = h.shape
    tm, tn = _tiles(S, D)
    n_exp = eu_bf16.shape[1]
    q, u_bf16 = _fused_linear(
        S=S, K=D, N=PEER_HEADS * 2 * N_KEYS, tm=tm, tn=tn, npass=NPASS_PEER_Q,
        pro_fn=_pro_norm_mod, pro_in=_norm_mod_inputs(h, norm_g, sc, sh, tm),
        w_in=[(w_q, _w_spec(D, tn))], epi_fn=_epi_plain, epi_in=[], emit_hi=True)

    gate_shape = jax.ShapeDtypeStruct((PEER_HEADS, N_KEYS, S), F32)
    gate_spec = pl.BlockSpec((PEER_HEADS, N_KEYS, tm), lambda i: (0, 0, i))
    rank2, e2, nrow, crow = pl.pallas_call(
        functools.partial(_peer_route_kernel, npass=NPASS_ROUTE),
        grid=(S // tm,),
        in_specs=[pl.BlockSpec((tm, PEER_HEADS * 2 * N_KEYS), lambda i: (i, 0)),
                  pl.BlockSpec(sub_keys.shape, lambda i: (0, 0, 0, 0))],
        out_specs=[gate_spec] * 4,
        out_shape=[gate_shape] * 4,
        compiler_params=_params(("arbitrary",)),
    )(q, sub_keys)

    keys_per_tile = PEER_EXPERT_TILE // N_KEYS
    res_spec = pl.BlockSpec((PEER_HEADS, N_KEYS, tm), lambda i, e: (0, 0, i))
    row_spec = pl.BlockSpec((PEER_HEADS, keys_per_tile, tm), lambda i, e: (0, e, i))
    return pl.pallas_call(
        _peer_dense_kernel,
        grid=(S // tm, n_exp // PEER_EXPERT_TILE),
        in_specs=[pl.BlockSpec((tm, D), lambda i, e: (i, 0)),
                  pl.BlockSpec((None, PEER_EXPERT_TILE, D), lambda i, e: (layer, e, 0)),
                  pl.BlockSpec((None, PEER_EXPERT_TILE, D), lambda i, e: (layer, e, 0)),
                  res_spec, res_spec, row_spec, row_spec,
                  pl.BlockSpec((tm, D), lambda i, e: (i, 0)),
                  pl.BlockSpec((1, D), lambda i, e: (0, 0))],
        out_specs=pl.BlockSpec((tm, D), lambda i, e: (i, 0)),
        out_shape=jax.ShapeDtypeStruct((S, D), F32),
        scratch_shapes=[pltpu.VMEM((tm, PEER_EXPERT_TILE), BF16)],
        compiler_params=_params(("arbitrary", "arbitrary")),
    )(u_bf16, eu_bf16, ev_bf16, rank2, e2, nrow, crow, h, gate)


def _final_norm_kernel(x_ref, g_ref, o_ref):
    o_ref[...] = _rms(x_ref[...], g_ref[...])


def _final_norm(h, g):
    S, D = h.shape
    tm = min(512, S)
    return pl.pallas_call(
        _final_norm_kernel,
        grid=(S // tm,),
        in_specs=[pl.BlockSpec((tm, D), lambda i: (i, 0)), pl.BlockSpec((1, D), lambda i: (0, 0))],
        out_specs=pl.BlockSpec((tm, D), lambda i: (i, 0)),
        out_shape=jax.ShapeDtypeStruct((S, D), F32),
        compiler_params=_params(("arbitrary",)),
    )(h, g.reshape(1, D))


def kernel(x, c, positions, ada_w, ada_b, norm_g, conv_w_pw1, conv_b_pw1, conv_w_dw, conv_b_dw, conv_ln_g, conv_ln_b, conv_w_pw2, conv_b_pw2, hg_w_in, hg_lb_logits, hg_norm_g, hg_w_out, mla_w_in, mla_q_norm_g, mla_kv_norm_g, mla_w_uq, mla_w_ukv, mla_w_o, peer_w_q, peer_sub_keys, peer_u, peer_v, final_g):
    B, S, D = x.shape
    assert B == 1
    depth = ada_w.shape[0]
    mod = _ada_mod(c, ada_w, ada_b)
    eu = peer_u.astype(BF16)
    ev = peer_v.astype(BF16)
    h = x.reshape(S, D)
    for i in range(depth):
        sh1, sc1, g1, sh2, sc2, g2 = [mod[i, :, k * D:(k + 1) * D] for k in range(6)]
        kind, slot = i % 3, i // 3
        if kind == 0:
            h = _conv_mixer(h, norm_g[i, 0], sc1, sh1, g1, conv_w_pw1[slot], conv_b_pw1[slot], conv_w_dw[slot],
                            conv_b_dw[slot], conv_ln_g[slot], conv_ln_b[slot], conv_w_pw2[slot], conv_b_pw2[slot])
        elif kind == 1:
            h = _hgrn_mixer(h, norm_g[i, 0], sc1, sh1, g1, hg_w_in[slot], hg_lb_logits, i, hg_norm_g[slot],
                            hg_w_out[slot])
        else:
            h = _mla_mixer(h, positions, norm_g[i, 0], sc1, sh1, g1, mla_w_in[slot], mla_q_norm_g[slot],
                           mla_kv_norm_g[slot], mla_w_uq[slot], mla_w_ukv[slot], mla_w_o[slot])
        h = _peer_mixer(h, norm_g[i, 1], sc2, sh2, g2, peer_w_q[i], peer_sub_keys[i], eu, ev, i)
    return _final_norm(h, final_g).reshape(B, S, D)
```

```python
import functools
import math

import jax
import jax.numpy as jnp
from jax import lax
from jax.experimental import pallas as pl
from jax.experimental.pallas import tpu as pltpu

F32 = jnp.float32
BF16 = jnp.bfloat16

CHUNK = 64
EPS = 1e-6
CONV_WIDTH = 31
HG_HEADS = 16
HG_DK = 128
MLA_HEADS = 16
Q_LORA = 512
KV_LORA = 512
QK_NOPE = 128
QK_ROPE = 64
V_HEAD = 128
ROPE_THETA = 10000.0
PEER_HEADS = 8
N_KEYS = 128
PEER_TOPK = 16

LANES = 128
SUBLANES = 8
MXU_DIM = 256
VMEM_LIMIT_BYTES = 56 * 1024 * 1024

NPASS_MIXER = 1
NPASS_PEER_Q = 1
NPASS_ROUTE = 3
SUB = 16

NN = (((1,), (0,)), ((), ()))
NT = (((1,), (1,)), ((), ()))
TN = (((0,), (0,)), ((), ()))


def _params(sem):
    return pltpu.CompilerParams(dimension_semantics=sem, vmem_limit_bytes=VMEM_LIMIT_BYTES)


def _split(x, n):
    hi = x.astype(BF16)
    if n == 1:
        return (hi,)
    r = x - hi.astype(F32)
    mid = r.astype(BF16)
    if n == 2:
        return (hi, mid)
    lo = (r - mid.astype(F32)).astype(BF16)
    return (hi, mid, lo)


def _nparts(npass):
    return 1 if npass == 1 else 2


def _mm(a_parts, b_parts, dn, npass):
    order = 0 if npass == 1 else 1
    out = None
    for i, a in enumerate(a_parts):
        for j, b in enumerate(b_parts):
            if i + j > order:
                continue
            t = lax.dot_general(a, b, dn, preferred_element_type=F32)
            out = t if out is None else out + t
    return out


def _rms(x, g):
    return x * lax.rsqrt(jnp.mean(x * x, axis=-1, keepdims=True) + EPS) * g


def _norm_mod(x, g, sc, sh):
    return _rms(x, g) * (1.0 + sc) + sh


def _mod_kernel(c_ref, w_ref, b_ref, o_ref):
    c = c_ref[...]
    cond = c * jax.nn.sigmoid(c)
    o_ref[0] = jnp.sum(w_ref[0] * cond, axis=0, keepdims=True) + b_ref[0]


def _ada_mod(c, ada_w, ada_b):
    depth, d, n = ada_w.shape
    tn = 1536 if n % 1536 == 0 else n
    return pl.pallas_call(
        _mod_kernel,
        grid=(depth, n // tn),
        in_specs=[pl.BlockSpec((d, 1), lambda i, j: (0, 0)),
                  pl.BlockSpec((1, d, tn), lambda i, j: (i, 0, j)),
                  pl.BlockSpec((1, 1, tn), lambda i, j: (i, 0, j))],
        out_specs=pl.BlockSpec((1, 1, tn), lambda i, j: (i, 0, j)),
        out_shape=jax.ShapeDtypeStruct((depth, 1, n), F32),
        compiler_params=_params(("arbitrary", "arbitrary")),
    )(c.reshape(d, 1), ada_w, ada_b.reshape(depth, 1, n))


def _linear_kernel(*refs, n_pro, n_w, n_epi, n_out, n_pscr, npass, pro_fn, epi_fn, emit_hi):
    np_ = _nparts(npass)
    pro_refs = refs[:n_pro]
    w_refs = refs[n_pro:n_pro + n_w]
    epi_refs = refs[n_pro + n_w:n_pro + n_w + n_epi]
    out_refs = refs[n_pro + n_w + n_epi:n_pro + n_w + n_epi + n_out]
    rest = refs[n_pro + n_w + n_epi + n_out:]
    hi_ref = None
    if emit_hi:
        hi_ref, rest = rest[0], rest[1:]
    u_scr = rest[:np_]
    p_scr = rest[np_:np_ + n_pscr]

    row_tile = pl.program_id(0)

    @pl.when(pl.program_id(1) == 0)
    def _():
        u = pro_fn(row_tile, *pro_refs, *p_scr)
        parts = _split(u, np_)
        for s, p in zip(u_scr, parts):
            s[...] = p
        if emit_hi:
            hi_ref[...] = parts[0]

    a_parts = tuple(s[...] for s in u_scr)
    accs = [_mm(a_parts, _split(w[...], np_), NN, npass) for w in w_refs]
    outs = epi_fn(accs, *epi_refs)
    for o_ref, o in zip(out_refs, outs):
        o_ref[...] = o


def _fused_linear(*, S, K, N, tm, tn, npass, pro_fn, pro_in, w_in, epi_fn, epi_in,
                  n_out=1, emit_hi=False, pro_scratch=()):
    np_ = _nparts(npass)
    arrays = [a for a, _ in pro_in] + [a for a, _ in w_in] + [a for a, _ in epi_in]
    specs = [s for _, s in pro_in] + [s for _, s in w_in] + [s for _, s in epi_in]
    out_shape = [jax.ShapeDtypeStruct((S, N), F32)] * n_out
    out_specs = [pl.BlockSpec((tm, tn), lambda i, j: (i, j))] * n_out
    if emit_hi:
        out_shape = out_shape + [jax.ShapeDtypeStruct((S, K), BF16)]
        out_specs = out_specs + [pl.BlockSpec((tm, K), lambda i, j: (i, 0))]
    kern = functools.partial(
        _linear_kernel, n_pro=len(pro_in), n_w=len(w_in), n_epi=len(epi_in), n_out=n_out,
        n_pscr=len(pro_scratch), npass=npass, pro_fn=pro_fn, epi_fn=epi_fn, emit_hi=emit_hi)
    return pl.pallas_call(
        kern,
        grid=(S // tm, N // tn),
        in_specs=specs,
        out_specs=out_specs,
        out_shape=out_shape,
        scratch_shapes=[pltpu.VMEM((tm, K), BF16)] * np_ + list(pro_scratch),
        compiler_params=_params(("arbitrary", "arbitrary")),
    )(*arrays)


def _row_spec(tm, k):
    return pl.BlockSpec((tm, k), lambda i, j: (i, 0))


def _vec_spec(k):
    return pl.BlockSpec((1, k), lambda i, j: (0, 0))


def _col_vec_spec(tn, off=0):
    return pl.BlockSpec((1, tn), lambda i, j: (0, j + off))


def _w_spec(k, tn, off=0):
    return pl.BlockSpec((k, tn), lambda i, j: (0, j + off))


def _tile_spec(tm, tn):
    return pl.BlockSpec((tm, tn), lambda i, j: (i, j))


def _pro_norm_mod(row_tile, x_ref, g_ref, sc_ref, sh_ref):
    return _norm_mod(x_ref[...], g_ref[...], sc_ref[...], sh_ref[...])


def _pro_identity(row_tile, x_ref):
    return x_ref[...]


def _epi_plain(accs):
    return (accs[0],)


def _epi_residual(accs, h_ref, g_ref):
    return (h_ref[...] + g_ref[...] * accs[0],)


def _epi_residual_bias(accs, b_ref, h_ref, g_ref):
    return (h_ref[...] + g_ref[...] * (accs[0] + b_ref[...]),)


def _epi_glu(accs, b1_ref, b2_ref):
    return ((accs[0] + b1_ref[...]) * jax.nn.sigmoid(accs[1] + b2_ref[...]),)


def _tiles(S, N):
    tm = min(512, S)
    tn = 512 if N % 512 == 0 else N
    return tm, tn


def _norm_mod_inputs(h, g, sc, sh, tm):
    d = h.shape[1]
    return [(h, _row_spec(tm, d)), (g.reshape(1, d), _vec_spec(d)), (sc, _vec_spec(d)), (sh, _vec_spec(d))]


CONV_HALO = 32
CONV_ROWS = 64


def _pro_conv(row_tile, z_ref, halo_ref, wdw_ref, bdw_ref, lng_ref, lnb_ref, zbuf, cbuf):
    tm, d = z_ref.shape
    first = row_tile == 0
    zbuf[0:CONV_HALO, :] = jnp.where(first, 0.0, halo_ref[...])
    zbuf[CONV_HALO:, :] = z_ref[...]
    lead = CONV_HALO - (CONV_WIDTH - 1)

    def col_body(cb, carry):
        c0 = pl.multiple_of(cb * LANES, LANES)
        cols = pl.ds(c0, LANES)
        for rc in range(tm // CONV_ROWS):
            acc = jnp.zeros((CONV_ROWS, LANES), F32)
            for w in range(CONV_WIDTH):
                acc = acc + zbuf[pl.ds(rc * CONV_ROWS + lead + w, CONV_ROWS), cols] * wdw_ref[pl.ds(w, 1), cols]
            cbuf[pl.ds(rc * CONV_ROWS, CONV_ROWS), cols] = acc + bdw_ref[:, cols]
        return carry

    lax.fori_loop(0, d // LANES, col_body, 0)
    y = cbuf[...]
    mu = jnp.mean(y, axis=-1, keepdims=True)
    var = jnp.mean(jnp.square(y - mu), axis=-1, keepdims=True)
    y = (y - mu) * lax.rsqrt(var + EPS) * lng_ref[...] + lnb_ref[...]
    return y * jax.nn.sigmoid(y)


def _conv_mixer(h, norm_g, sc, sh, gate, w_pw1, b_pw1, w_dw, b_dw, ln_g, ln_b, w_pw2, b_pw2):
    S, D = h.shape
    tm, tn = _tiles(S, D)
    nb = D // tn
    z = _fused_linear(
        S=S, K=D, N=D, tm=tm, tn=tn, npass=NPASS_MIXER,
        pro_fn=_pro_norm_mod, pro_in=_norm_mod_inputs(h, norm_g, sc, sh, tm),
        w_in=[(w_pw1, _w_spec(D, tn)), (w_pw1, _w_spec(D, tn, nb))],
        epi_fn=_epi_glu,
        epi_in=[(b_pw1.reshape(1, 2 * D), _col_vec_spec(tn)), (b_pw1.reshape(1, 2 * D), _col_vec_spec(tn, nb))])[0]
    halo_blocks = tm // CONV_HALO
    halo_spec = pl.BlockSpec((CONV_HALO, D), lambda i, j: (jnp.maximum(i * halo_blocks - 1, 0), 0))
    return _fused_linear(
        S=S, K=D, N=D, tm=tm, tn=tn, npass=NPASS_MIXER,
        pro_fn=_pro_conv,
        pro_in=[(z, _row_spec(tm, D)), (z, halo_spec),
                (w_dw, pl.BlockSpec((CONV_WIDTH, D), lambda i, j: (0, 0))),
                (b_dw.reshape(1, D), _vec_spec(D)), (ln_g.reshape(1, D), _vec_spec(D)),
                (ln_b.reshape(1, D), _vec_spec(D))],
        pro_scratch=[pltpu.VMEM((tm + CONV_HALO, D), F32), pltpu.VMEM((tm, D), F32)],
        w_in=[(w_pw2, _w_spec(D, tn))],
        epi_fn=_epi_residual_bias,
        epi_in=[(b_pw2.reshape(1, D), _col_vec_spec(tn)), (h, _tile_spec(tm, tn)), (gate, _col_vec_spec(tn))])[0]


def _hgrn_scan_kernel(q_ref, f_ref, i_ref, g_ref, logit_ref, ng_ref, o_ref, st_ref, *, rows, layer, npass):
    np_ = _nparts(npass)

    @pl.when(pl.program_id(1) == 0)
    def _():
        st_ref[...] = jnp.zeros_like(st_ref)

    lg = logit_ref[...]
    e = jnp.exp(lg - jnp.max(lg, axis=0, keepdims=True))
    p = e / jnp.sum(e, axis=0, keepdims=True)
    cs = p[0:1]
    for j in range(1, layer + 1):
        cs = cs + p[j:j + 1]
    lb = cs - p[0:1]
    log_lb = jnp.log(lb)
    log_1m_lb = jnp.log1p(-lb)

    r_i = lax.broadcasted_iota(jnp.int32, (CHUNK, CHUNK), 0)
    c_i = lax.broadcasted_iota(jnp.int32, (CHUNK, CHUNK), 1)
    tri = (r_i >= c_i).astype(BF16)
    col_sub = lax.broadcasted_iota(jnp.int32, (SUB, CHUNK), 1)
    row_mod = lax.broadcasted_iota(jnp.int32, (CHUNK, 1), 0) % SUB
    ng = ng_ref[...]

    def chunk_body(c, carry):
        rws = pl.ds(pl.multiple_of(c * CHUNK, CHUNK), CHUNK)
        q = q_ref[rws, :] * (HG_DK ** -0.5)
        f = f_ref[rws, :]
        v = i_ref[rws, :]
        g = g_ref[rws, :]
        log_sig = jnp.minimum(f, 0.0) - jnp.log1p(jnp.exp(-jnp.abs(f)))
        t2 = log_1m_lb + log_sig
        lf = jnp.maximum(log_lb, t2) + jnp.log1p(jnp.exp(-jnp.abs(log_lb - t2)))
        k = 1.0 - jnp.exp(lf)
        b = None
        for part in _split(lf, 3):
            t = lax.dot_general(tri, part, NN, preferred_element_type=F32)
            b = t if b is None else b + t
        st = st_ref[...]
        o = _mm(_split(q * jnp.exp(b), np_), _split(st, np_), NT, npass)
        pieces = [jnp.zeros((SUB, CHUNK), F32)]
        for blk in range(1, CHUNK // SUB):
            lo = blk * SUB
            ref = b[lo - 1:lo, :]
            q_b = q[lo:lo + SUB, :] * jnp.exp(b[lo:lo + SUB, :] - ref)
            k_b = k * jnp.exp(jnp.minimum(ref - b, 0.0))
            a_b = _mm(_split(q_b, np_), _split(k_b, np_), NT, npass)
            pieces.append(jnp.where(col_sub < lo, a_b, 0.0))
        attn = jnp.concatenate(pieces, axis=0)
        o = o + _mm(_split(attn, np_), _split(v, np_), NN, npass)
        for lag in range(SUB):
            if lag == 0:
                ks, bs, vs = k, b, v
            else:
                ks = pltpu.roll(k, lag, 0)
                bs = pltpu.roll(b, lag, 0)
                vs = pltpu.roll(v, lag, 0)
            valid = row_mod >= lag
            dec = jnp.exp(jnp.where(valid, b - bs, 0.0))
            cl = jnp.sum(q * ks * dec, axis=-1, keepdims=True)
            o = o + jnp.where(valid, cl, 0.0) * vs
        b_last = b[CHUNK - 1:CHUNK, :]
        k_dec = k * jnp.exp(b_last - b)
        st_ref[...] = st * jnp.exp(b_last) + _mm(_split(v, np_), _split(k_dec, np_), TN, npass)
        o_ref[rws, :] = _rms(o, ng) * (g * jax.nn.sigmoid(g))
        return carry

    lax.fori_loop(0, rows // CHUNK, chunk_body, 0)


def _hgrn_mixer(h, norm_g, sc, sh, gate, w_in, lb_logits, layer, hg_norm_g, w_out):
    S, D = h.shape
    tm, tn = _tiles(S, D)
    y4 = _fused_linear(
        S=S, K=D, N=4 * D, tm=tm, tn=tn, npass=NPASS_MIXER,
        pro_fn=_pro_norm_mod, pro_in=_norm_mod_inputs(h, norm_g, sc, sh, tm),
        w_in=[(w_in, _w_spec(D, tn))], epi_fn=_epi_plain, epi_in=[])[0]
    rows = min(512, S)
    depth = lb_logits.shape[0]
    nh = D // HG_DK

    def part_spec(part):
        return pl.BlockSpec((rows, HG_DK), lambda hd, r: (r, part * nh + hd))

    og = pl.pallas_call(
        functools.partial(_hgrn_scan_kernel, rows=rows, layer=layer, npass=NPASS_MIXER),
        grid=(nh, S // rows),
        in_specs=[part_spec(0), part_spec(1), part_spec(2), part_spec(3),
                  pl.BlockSpec((depth, HG_DK), lambda hd, r: (0, hd)),
                  pl.BlockSpec((1, HG_DK), lambda hd, r: (0, 0))],
        out_specs=pl.BlockSpec((rows, HG_DK), lambda hd, r: (r, hd)),
        out_shape=jax.ShapeDtypeStruct((S, D), F32),
        scratch_shapes=[pltpu.VMEM((HG_DK, HG_DK), F32)],
        compiler_params=_params(("arbitrary", "arbitrary")),
    )(y4, y4, y4, y4, lb_logits, hg_norm_g.reshape(1, HG_DK))
    return _fused_linear(
        S=S, K=D, N=D, tm=tm, tn=tn, npass=NPASS_MIXER,
        pro_fn=_pro_identity, pro_in=[(og, _row_spec(tm, D))],
        w_in=[(w_out, _w_spec(D, tn))], epi_fn=_epi_residual,
        epi_in=[(h, _tile_spec(tm, tn)), (gate, _col_vec_spec(tn))])[0]


HEAD_PAD = 2 * LANES
C_PAD = Q_LORA + KV_LORA + LANES


def _swap_rope_halves(x):
    half = QK_ROPE // 2
    lane = lax.broadcasted_iota(jnp.int32, x.shape, 1)
    return jnp.where(lane < half, pltpu.roll(x, LANES - half, 1), pltpu.roll(x, half, 1))


def _mla_proj_kernel(*refs, npass, scale):
    np_ = _nparts(npass)
    c_ref, pos_ref, invf_ref, sgn_ref, qg_ref, kvg_ref, wq_ref, wkv_ref = refs[:8]
    outs = refs[8:8 + 3 * np_]
    q_out, k_out, v_out = outs[:np_], outs[np_:2 * np_], outs[2 * np_:]
    scr = refs[8 + 3 * np_:]
    cq_scr, ckv_scr = scr[:np_], scr[np_:2 * np_]
    cos_scr, sin_scr, kr_scr = scr[2 * np_:]

    @pl.when(pl.program_id(1) == 0)
    def _():
        c = c_ref[...]
        for s, p in zip(cq_scr, _split(_rms(c[:, :Q_LORA], qg_ref[...]), np_)):
            s[...] = p
        for s, p in zip(ckv_scr, _split(_rms(c[:, Q_LORA:Q_LORA + KV_LORA], kvg_ref[...]), np_)):
            s[...] = p
        ang = pos_ref[...].astype(F32) * invf_ref[...]
        cos = jnp.cos(ang)
        sin = jnp.sin(ang) * sgn_ref[...]
        cos_scr[...] = cos
        sin_scr[...] = sin
        kr = c[:, Q_LORA + KV_LORA:]
        kr_scr[...] = kr * cos + _swap_rope_halves(kr) * sin

    cq = tuple(s[...] for s in cq_scr)
    ckv = tuple(s[...] for s in ckv_scr)
    rq = _mm(cq, _split(wq_ref[...], np_), NN, npass)
    q2 = rq[:, LANES:]
    q2 = q2 * cos_scr[...] + _swap_rope_halves(q2) * sin_scr[...]
    rkv = _mm(ckv, _split(wkv_ref[...], np_), NN, npass)
    for o, a, b in zip(q_out, _split(rq[:, :LANES] * scale, np_), _split(q2 * scale, np_)):
        o[:, :LANES] = a
        o[:, LANES:] = b
    for o, a, b in zip(k_out, _split(rkv[:, :LANES], np_), _split(kr_scr[...], np_)):
        o[:, :LANES] = a
        o[:, LANES:] = b
    for o, a in zip(v_out, _split(rkv[:, LANES:], np_)):
        o[...] = a


def _attn_kernel(*refs, npass, tq):
    np_ = _nparts(npass)
    q_refs, k_refs, v_refs = refs[:np_], refs[np_:2 * np_], refs[2 * np_:3 * np_]
    o_ref, m_scr, l_scr, acc_scr = refs[3 * np_:]
    qi = pl.program_id(1)
    ki = pl.program_id(2)

    @pl.when(ki == 0)
    def _():
        m_scr[...] = jnp.full_like(m_scr, -jnp.inf)
        l_scr[...] = jnp.zeros_like(l_scr)
        acc_scr[...] = jnp.zeros_like(acc_scr)

    def step(masked):
        s = _mm(tuple(r[...] for r in q_refs), tuple(r[...] for r in k_refs), NT, npass)
        if masked:
            r_c = lax.broadcasted_iota(jnp.int32, s.shape, 0) // CHUNK
            c_c = lax.broadcasted_iota(jnp.int32, s.shape, 1) // CHUNK
            s = jnp.where(c_c <= r_c, s, -jnp.inf)
        m_prev = m_scr[...]
        m_new = jnp.maximum(m_prev, jnp.max(s, axis=-1, keepdims=True))
        alpha = jnp.exp(m_prev - m_new)
        p = jnp.exp(s - m_new)
        l_scr[...] = alpha * l_scr[...] + jnp.sum(p, axis=-1, keepdims=True)
        acc_scr[...] = alpha * acc_scr[...] + _mm(_split(p, np_), tuple(r[...] for r in v_refs), NN, npass)
        m_scr[...] = m_new

    @pl.when(ki < qi)
    def _():
        step(False)

    @pl.when(ki == qi)
    def _():
        step(True)
        o_ref[...] = acc_scr[...] / l_scr[...]


def _mla_mixer(h, pos, norm_g, sc, sh, gate, w_in, q_norm_g, kv_norm_g, w_uq, w_ukv, w_o):
    S, D = h.shape
    tm, tn = _tiles(S, D)
    npass = NPASS_MIXER
    np_ = _nparts(npass)
    nh = MLA_HEADS
    w_in_p = jnp.pad(w_in, ((0, 0), (0, C_PAD - w_in.shape[1])))
    w_uq_p = jnp.pad(w_uq.reshape(Q_LORA, nh, QK_NOPE + QK_ROPE),
                     ((0, 0), (0, 0), (0, HEAD_PAD - QK_NOPE - QK_ROPE))).reshape(Q_LORA, nh * HEAD_PAD)
    c_lat = _fused_linear(
        S=S, K=D, N=C_PAD, tm=tm, tn=C_PAD, npass=npass,
        pro_fn=_pro_norm_mod, pro_in=_norm_mod_inputs(h, norm_g, sc, sh, tm),
        w_in=[(w_in_p, _w_spec(D, C_PAD))], epi_fn=_epi_plain, epi_in=[])[0]

    inv_freq = ROPE_THETA ** (-jnp.arange(0, QK_ROPE, 2, dtype=F32) / QK_ROPE)
    zeros = jnp.zeros((LANES - QK_ROPE,), F32)
    invf = jnp.concatenate([inv_freq, inv_freq, zeros]).reshape(1, LANES)
    half = QK_ROPE // 2
    sgn = jnp.concatenate([-jnp.ones((half,), F32), jnp.ones((half,), F32), zeros]).reshape(1, LANES)
    scale = (QK_NOPE + QK_ROPE) ** -0.5

    qkv = pl.pallas_call(
        functools.partial(_mla_proj_kernel, npass=npass, scale=scale),
        grid=(S // tm, nh),
        in_specs=[pl.BlockSpec((tm, C_PAD), lambda i, j: (i, 0)),
                  pl.BlockSpec((tm, 1), lambda i, j: (i, 0)),
                  pl.BlockSpec((1, LANES), lambda i, j: (0, 0)),
                  pl.BlockSpec((1, LANES), lambda i, j: (0, 0)),
                  pl.BlockSpec((1, Q_LORA), lambda i, j: (0, 0)),
                  pl.BlockSpec((1, KV_LORA), lambda i, j: (0, 0)),
                  pl.BlockSpec((Q_LORA, HEAD_PAD), lambda i, j: (0, j)),
                  pl.BlockSpec((KV_LORA, QK_NOPE + V_HEAD), lambda i, j: (0, j))],
        out_specs=([pl.BlockSpec((tm, HEAD_PAD), lambda i, j: (i, j))] * (2 * np_)
                   + [pl.BlockSpec((tm, V_HEAD), lambda i, j: (i, j))] * np_),
        out_shape=([jax.ShapeDtypeStruct((S, nh * HEAD_PAD), BF16)] * (2 * np_)
                   + [jax.ShapeDtypeStruct((S, nh * V_HEAD), BF16)] * np_),
        scratch_shapes=([pltpu.VMEM((tm, Q_LORA), BF16)] * np_ + [pltpu.VMEM((tm, KV_LORA), BF16)] * np_
                        + [pltpu.VMEM((tm, LANES), F32)] * 3),
        compiler_params=_params(("arbitrary", "arbitrary")),
    )(c_lat, pos.reshape(S, 1), invf, sgn, q_norm_g.reshape(1, Q_LORA), kv_norm_g.reshape(1, KV_LORA),
      w_uq_p, w_ukv)
    q_parts, k_parts, v_parts = qkv[:np_], qkv[np_:2 * np_], qkv[2 * np_:]

    tq = min(1024, S)
    o = pl.pallas_call(
        functools.partial(_attn_kernel, npass=npass, tq=tq),
        grid=(nh, S // tq, S // tq),
        in_specs=([pl.BlockSpec((tq, HEAD_PAD), lambda hd, qi, ki: (qi, hd))] * np_
                  + [pl.BlockSpec((tq, HEAD_PAD), lambda hd, qi, ki: (jnp.minimum(ki, qi), hd))] * np_
                  + [pl.BlockSpec((tq, V_HEAD), lambda hd, qi, ki: (jnp.minimum(ki, qi), hd))] * np_),
        out_specs=pl.BlockSpec((tq, V_HEAD), lambda hd, qi, ki: (qi, hd)),
        out_shape=jax.ShapeDtypeStruct((S, nh * V_HEAD), F32),
        scratch_shapes=[pltpu.VMEM((tq, 1), F32), pltpu.VMEM((tq, 1), F32), pltpu.VMEM((tq, V_HEAD), F32)],
        compiler_params=_params(("arbitrary", "arbitrary", "arbitrary")),
    )(*q_parts, *k_parts, *v_parts)

    return _fused_linear(
        S=S, K=nh * V_HEAD, N=D, tm=tm, tn=tn, npass=npass,
        pro_fn=_pro_identity, pro_in=[(o, _row_spec(tm, nh * V_HEAD))],
        w_in=[(w_o, _w_spec(nh * V_HEAD, tn))], epi_fn=_epi_residual,
        epi_in=[(h, _tile_spec(tm, tn)), (gate, _col_vec_spec(tn))])[0]


NOT_RANKED = 99.0
PAIR_ROWS = 8


def _top16(s, row_id):
    x = s
    rank = jnp.full(s.shape, NOT_RANKED, F32)
    rank_id = lax.broadcasted_iota(jnp.int32, (PEER_TOPK, s.shape[1]), 0)
    stack = jnp.zeros((PEER_TOPK, s.shape[1]), F32)
    vals = []
    for a in range(PEER_TOPK):
        m = jnp.max(x, axis=0, keepdims=True)
        first = jnp.min(jnp.where(x == m, row_id, 1e9), axis=0, keepdims=True)
        hit = row_id == first
        rank = jnp.where(hit, float(a), rank)
        x = jnp.where(hit, -jnp.inf, x)
        stack = jnp.where(rank_id == a, m, stack)
        vals.append(m)
    return vals, stack, rank


def _peer_route_kernel(q_ref, keys_ref, rank2_ref, e2_ref, nrow_ref, crow_ref, *, npass):
    np_ = _nparts(npass)
    tmr = q_ref.shape[0]
    row_id = lax.broadcasted_iota(jnp.int32, (N_KEYS, tmr), 0).astype(F32)
    flat_rows = ([float(b) for b in range(PEER_TOPK)]
                 + [float(a * PEER_TOPK + b) for a in range(1, PAIR_ROWS) for b in range(PAIR_ROWS)]
                 + [float(a * PEER_TOPK) for a in range(PAIR_ROWS, PEER_TOPK)])
    n_cand = len(flat_rows)
    cand_row = lax.broadcasted_iota(jnp.int32, (n_cand, 1), 0)
    flat_id = jnp.zeros((n_cand, 1), F32)
    for r, fv in enumerate(flat_rows):
        flat_id = jnp.where(cand_row == r, fv, flat_id)

    for hd in range(PEER_HEADS):
        halves = []
        for half in range(2):
            col = (2 * hd + half) * N_KEYS
            qh = q_ref[:, col:col + N_KEYS]
            s = _mm(_split(keys_ref[hd, half], np_), _split(qh, np_), NT, npass)
            halves.append((s,) + _top16(s, row_id))
        (s1, v1, t1, rank1), (s2, v2, t2, rank2) = halves
        pieces = [v1[0] + t2]
        for a in range(1, PAIR_ROWS):
            pieces.append(v1[a] + t2[0:PAIR_ROWS])
        pieces.append(t1[PAIR_ROWS:] + v2[0])
        cand = jnp.concatenate(pieces, axis=0)
        x = cand
        for _ in range(PEER_TOPK):
            m = jnp.max(x, axis=0, keepdims=True)
            first = jnp.min(jnp.where(x == m, flat_id, 1e9), axis=0, keepdims=True)
            x = jnp.where(flat_id == first, -jnp.inf, x)
        sel = x == -jnp.inf
        z = jnp.sum(jnp.where(sel, jnp.exp(cand - (v1[0] + v2[0])), 0.0), axis=0, keepdims=True)
        self_f = sel.astype(F32)
        nsel = [jnp.sum(self_f[0:PEER_TOPK], axis=0, keepdims=True)]
        for a in range(1, PAIR_ROWS):
            lo = PEER_TOPK + (a - 1) * PAIR_ROWS
            nsel.append(jnp.sum(self_f[lo:lo + PAIR_ROWS], axis=0, keepdims=True))
        tail = PEER_TOPK + (PAIR_ROWS - 1) * PAIR_ROWS
        for a in range(PAIR_ROWS, PEER_TOPK):
            nsel.append(self_f[tail + a - PAIR_ROWS:tail + a - PAIR_ROWS + 1])
        nrow = jnp.zeros((N_KEYS, tmr), F32)
        for a in range(PEER_TOPK):
            nrow = jnp.where(rank1 == float(a), nsel[a], nrow)
        rank2_ref[hd] = rank2
        e2_ref[hd] = jnp.exp(s2 - v2[0])
        nrow_ref[hd] = nrow
        crow_ref[hd] = jnp.exp(s1 - v1[0]) / z


PEER_EXPERT_TILE = 1024


def _peer_dense_kernel(u_ref, eu_ref, ev_ref, rank2_ref, e2_ref, nrow_ref, crow_ref, h_ref, g_ref,
                       o_ref, w_scr):
    e = pl.program_id(1)

    @pl.when(e == 0)
    def _():
        o_ref[...] = jnp.zeros_like(o_ref)

    u = u_ref[...]
    keys_per_sub = MXU_DIM // N_KEYS
    for sub in range(PEER_EXPERT_TILE // MXU_DIM):
        a = lax.dot_general(u, eu_ref[sub * MXU_DIM:(sub + 1) * MXU_DIM, :], NT,
                            preferred_element_type=F32)
        gts = []
        for ii in range(keys_per_sub):
            i1 = sub * keys_per_sub + ii
            acc = None
            for hd in range(PEER_HEADS):
                nr = nrow_ref[hd, i1:i1 + 1, :]
                cr = crow_ref[hd, i1:i1 + 1, :]
                t = jnp.where(rank2_ref[hd] < nr, e2_ref[hd] * cr, 0.0)
                acc = t if acc is None else acc + t
            gts.append(acc)
        gate = jnp.concatenate(gts, axis=0).T
        w_scr[:, sub * MXU_DIM:(sub + 1) * MXU_DIM] = (jax.nn.gelu(a, approximate=True) * gate).astype(BF16)
    o_ref[...] += lax.dot_general(w_scr[...], ev_ref[...], NN, preferred_element_type=F32)

    @pl.when(e == pl.num_programs(1) - 1)
    def _():
        o_ref[...] = h_ref[...] + g_ref[...] * o_ref[...]


def _peer_mixer(h, norm_g, sc, sh, gate, w_q, sub_keys, eu_bf16, ev_bf16, layer):
    S, D = h.shape
    tm, tn = _tiles(S, D)
    n_exp = eu_bf16.shape[1]
    q, u_bf16 = _fused_linear(
        S=S, K=D, N=PEER_HEADS * 2 * N_KEYS, tm=tm, tn=tn, npass=NPASS_PEER_Q,
        pro_fn=_pro_norm_mod, pro_in=_norm_mod_inputs(h, norm_g, sc, sh, tm),
        w_in=[(w_q, _w_spec(D, tn))], epi_fn=_epi_plain, epi_in=[], emit_hi=True)

    gate_shape = jax.ShapeDtypeStruct((PEER_HEADS, N_KEYS, S), F32)
    gate_spec = pl.BlockSpec((PEER_HEADS, N_KEYS, tm), lambda i: (0, 0, i))
    rank2, e2, nrow, crow = pl.pallas_call(
        functools.partial(_peer_route_kernel, npass=NPASS_ROUTE),
        grid=(S // tm,),
        in_specs=[pl.BlockSpec((tm, PEER_HEADS * 2 * N_KEYS), lambda i: (i, 0)),
                  pl.BlockSpec(sub_keys.shape, lambda i: (0, 0, 0, 0))],
        out_specs=[gate_spec] * 4,
        out_shape=[gate_shape] * 4,
        compiler_params=_params(("arbitrary",)),
    )(q, sub_keys)

    keys_per_tile = PEER_EXPERT_TILE // N_KEYS
    res_spec = pl.BlockSpec((PEER_HEADS, N_KEYS, tm), lambda i, e: (0, 0, i))
    row_spec = pl.BlockSpec((PEER_HEADS, keys_per_tile, tm), lambda i, e: (0, e, i))
    return pl.pallas_call(
        _peer_dense_kernel,
        grid=(S // tm, n_exp // PEER_EXPERT_TILE),
        in_specs=[pl.BlockSpec((tm, D), lambda i, e: (i, 0)),
                  pl.BlockSpec((None, PEER_EXPERT_TILE, D), lambda i, e: (layer, e, 0)),
                  pl.BlockSpec((None, PEER_EXPERT_TILE, D), lambda i, e: (layer, e, 0)),
                  res_spec, res_spec, row_spec, row_spec,
                  pl.BlockSpec((tm, D), lambda i, e: (i, 0)),
                  pl.BlockSpec((1, D), lambda i, e: (0, 0))],
        out_specs=pl.BlockSpec((tm, D), lambda i, e: (i, 0)),
        out_shape=jax.ShapeDtypeStruct((S, D), F32),
        scratch_shapes=[pltpu.VMEM((tm, PEER_EXPERT_TILE), BF16)],
        compiler_params=_params(("arbitrary", "arbitrary")),
    )(u_bf16, eu_bf16, ev_bf16, rank2, e2, nrow, crow, h, gate)


def _final_norm_kernel(x_ref, g_ref, o_ref):
    o_ref[...] = _rms(x_ref[...], g_ref[...])


def _final_norm(h, g):
    S, D = h.shape
    tm = min(512, S)
    return pl.pallas_call(
        _final_norm_kernel,
        grid=(S // tm,),
        in_specs=[pl.BlockSpec((tm, D), lambda i: (i, 0)), pl.BlockSpec((1, D), lambda i: (0, 0))],
        out_specs=pl.BlockSpec((tm, D), lambda i: (i, 0)),
        out_shape=jax.ShapeDtypeStruct((S, D), F32),
        compiler_params=_params(("arbitrary",)),
    )(h, g.reshape(1, D))


def kernel(x, c, positions, ada_w, ada_b, norm_g, conv_w_pw1, conv_b_pw1, conv_w_dw, conv_b_dw, conv_ln_g, conv_ln_b, conv_w_pw2, conv_b_pw2, hg_w_in, hg_lb_logits, hg_norm_g, hg_w_out, mla_w_in, mla_q_norm_g, mla_kv_norm_g, mla_w_uq, mla_w_ukv, mla_w_o, peer_w_q, peer_sub_keys, peer_u, peer_v, final_g):
    B, S, D = x.shape
    assert B == 1
    depth = ada_w.shape[0]
    mod = _ada_mod(c, ada_w, ada_b)
    eu = peer_u.astype(BF16)
    ev = peer_v.astype(BF16)
    h = x.reshape(S, D)
    for i in range(depth):
        sh1, sc1, g1, sh2, sc2, g2 = [mod[i, :, k * D:(k + 1) * D] for k in range(6)]
        kind, slot = i % 3, i // 3
        if kind == 0:
            h = _conv_mixer(h, norm_g[i, 0], sc1, sh1, g1, conv_w_pw1[slot], conv_b_pw1[slot], conv_w_dw[slot],
                            conv_b_dw[slot], conv_ln_g[slot], conv_ln_b[slot], conv_w_pw2[slot], conv_b_pw2[slot])
        elif kind == 1:
            h = _hgrn_mixer(h, norm_g[i, 0], sc1, sh1, g1, hg_w_in[slot], hg_lb_logits, i, hg_norm_g[slot],
                            hg_w_out[slot])
        else:
            h = _mla_mixer(h, positions, norm_g[i, 0], sc1, sh1, g1, mla_w_in[slot], mla_q_norm_g[slot],
                           mla_kv_norm_g[slot], mla_w_uq[slot], mla_w_ukv[slot], mla_w_o[slot])
        h = _peer_mixer(h, norm_g[i, 1], sc2, sh2, g2, peer_w_q[i], peer_sub_keys[i], eu, ev, i)
    return _final_norm(h, final_g).reshape(B, S, D)
```
